```python
import math
import jax, jax.numpy as jnp
from jax import lax
import numpy as np

D_MODEL = 1024
BATCH = 8
SEQ = 4096
DEPTH = 4

N_MEM = 256
D_MIX = D_MODEL
RET_WIDTH = D_MIX // 2
RET_HEADS = 4
RET_HEAD_DIM = RET_WIDTH // RET_HEADS
POOL_WIDTH = D_MIX - RET_WIDTH
POOL_WINDOWS = (2, 4, 8, 16)
POOL_GROUP = POOL_WIDTH // len(POOL_WINDOWS)
IN_COLS = 4 * RET_WIDTH + POOL_WIDTH
CHUNK = 128
ROPE_BASE = 10000.0
XA_HEADS = 4
XA_HEAD_DIM = D_MODEL // XA_HEADS
D_FF = ((8 * D_MODEL // 3 + 255) // 256) * 256
CONV_WIDTH = 3
EPS = 1e-6

kernel_name = "hybrid_retention_pool_trunk"


def rmsnorm(x, g):
    x32 = x.astype(jnp.float32)
    y = x32 * lax.rsqrt(jnp.mean(x32 * x32, axis=-1, keepdims=True) + EPS)
    return (y * g.astype(jnp.float32)).astype(x.dtype)


def rope(t, positions):
    dh = t.shape[-1]
    inv_freq = jnp.exp(-math.log(ROPE_BASE) * jnp.arange(0, dh, 2, dtype=jnp.float32) / dh)
    ang = positions.astype(jnp.float32)[..., None] * inv_freq
    cos = jnp.cos(ang)[:, :, None, :].astype(t.dtype)
    sin = jnp.sin(ang)[:, :, None, :].astype(t.dtype)
    t1, t2 = jnp.split(t, 2, axis=-1)
    return jnp.concatenate([t1 * cos - t2 * sin, t2 * cos + t1 * sin], axis=-1)


def retention(q, k, v, positions):
    B, S, _ = q.shape
    H, Dh = RET_HEADS, RET_HEAD_DIM
    dt = q.dtype
    N = S // CHUNK
    q = rope(q.reshape(B, S, H, Dh), positions)
    k = rope(k.reshape(B, S, H, Dh), positions) * (Dh ** -0.5)
    v = v.reshape(B, S, H, Dh)

    def to_chunks(t):
        return t.reshape(B, N, CHUNK, H, Dh).transpose(0, 3, 1, 2, 4)

    qc, kc, vc = to_chunks(q), to_chunks(k), to_chunks(v)
    lg = jnp.log1p(-jnp.exp2(-5.0 - jnp.arange(H, dtype=jnp.float32)))
    idx = jnp.arange(CHUNK, dtype=jnp.float32)
    diff = idx[:, None] - idx[None, :]
    dmat = jnp.where(diff >= 0, jnp.exp(lg[:, None, None] * jnp.maximum(diff, 0.0)), 0.0).astype(dt)
    zeta = jnp.exp(lg[:, None] * (CHUNK - 1 - idx)).astype(dt)
    xi = jnp.exp(lg[:, None] * (idx + 1.0)).astype(dt)
    chunk_decay = jnp.exp(lg * CHUNK).astype(dt)

    scores = jnp.einsum('bhnid,bhnjd->bhnij', qc, kc) * dmat[:, None]
    intra = jnp.einsum('bhnij,bhnjv->bhniv', scores, vc)
    kv = jnp.einsum('bhnjd,bhnjv->nbhdv', kc * zeta[:, None, :, None], vc)

    def step(state, kv_n):
        return state * chunk_decay[None, :, None, None] + kv_n, state

    _, states_before = lax.scan(step, jnp.zeros((B, H, Dh, Dh), dt), kv)
    cross = jnp.einsum('bhnid,nbhdv->bhniv', qc, states_before) * xi[:, None, :, None]
    return (intra + cross).transpose(0, 2, 3, 1, 4).reshape(B, S, H, Dh)


def head_rmsnorm(y, g):
    B, S, H, Dh = y.shape
    y32 = y.astype(jnp.float32)
    y32 = y32 * lax.rsqrt(jnp.mean(y32 * y32, axis=-1, keepdims=True) + EPS)
    return (y32.reshape(B, S, H * Dh) * g.astype(jnp.float32)).astype(y.dtype)


def pool_mixer(u, w, scale):
    B, S, _ = u.shape
    u32 = u.astype(jnp.float32)
    cs = jnp.cumsum(u32, axis=1)
    t1 = jnp.arange(1, S + 1, dtype=jnp.int32)
    groups = []
    for gi, wlen in enumerate(POOL_WINDOWS):
        c = cs[:, :, gi * POOL_GROUP:(gi + 1) * POOL_GROUP]
        lower = jnp.concatenate([jnp.zeros((B, wlen, POOL_GROUP), jnp.float32), c[:, :S - wlen]], axis=1)
        count = jnp.minimum(t1, wlen).astype(jnp.float32)[None, :, None]
        mean = (c - lower) / count
        groups.append(mean - u32[:, :, gi * POOL_GROUP:(gi + 1) * POOL_GROUP])
    pooled = jnp.stack(groups, axis=2).astype(u.dtype)
    out = jnp.einsum('bsgc,gcd->bsgd', pooled, w).reshape(B, S, POOL_WIDTH)
    return out * scale


def cross_attn(h, mem_n, wq, wkv, wo):
    B, S, _ = h.shape
    q = (h @ wq).reshape(B, S, XA_HEADS, XA_HEAD_DIM)
    k, v = jnp.split(mem_n @ wkv, 2, axis=-1)
    k = k.reshape(B, N_MEM, XA_HEADS, XA_HEAD_DIM)
    v = v.reshape(B, N_MEM, XA_HEADS, XA_HEAD_DIM)
    s = jnp.einsum('bshd,bmhd->bhsm', q, k).astype(jnp.float32) * (XA_HEAD_DIM ** -0.5)
    p = jax.nn.softmax(s, axis=-1).astype(v.dtype)
    o = jnp.einsum('bhsm,bmhd->bshd', p, v).reshape(B, S, D_MODEL)
    return o @ wo


def conv_glu_ffn(h, w_up, conv_w, conv_b, w_down):
    a = h @ w_up
    S = a.shape[1]
    a1 = jnp.pad(a, ((0, 0), (1, 0), (0, 0)))[:, :S]
    a2 = jnp.pad(a, ((0, 0), (2, 0), (0, 0)))[:, :S]
    c = conv_w[0] * a2 + conv_w[1] * a1 + conv_w[2] * a + conv_b
    gate, val = jnp.split(c, 2, axis=-1)
    return (jax.nn.silu(gate) * val) @ w_down


def setup_inputs(seed: int = 0) -> dict:
    key = jax.random.key(seed)
    ks = jax.random.split(key, 24)

    def nrm(k, shape, scale):
        return jax.random.normal(k, shape, jnp.float32) * scale

    L = DEPTH
    x = nrm(ks[0], (BATCH, SEQ, D_MODEL), 1.0)
    mem = nrm(ks[1], (BATCH, N_MEM, D_MODEL), 1.0)
    offs = jax.random.randint(ks[2], (BATCH, 1), 0, 1024, dtype=jnp.int32)
    positions = (offs + jnp.arange(SEQ, dtype=jnp.int32)[None, :]).astype(jnp.int32)
    return {
        "x": x,
        "mem": mem,
        "positions": positions,
        "mix_norm_g": 1.0 + nrm(ks[3], (L, D_MODEL), 0.02),
        "w_in": nrm(ks[4], (L, D_MODEL, IN_COLS), D_MODEL ** -0.5),
        "ret_gn_g": 1.0 + nrm(ks[5], (L, RET_WIDTH), 0.02),
        "pool_w": nrm(ks[6], (L, len(POOL_WINDOWS), POOL_GROUP, POOL_GROUP), POOL_GROUP ** -0.5),
        "pool_scale": 0.5 + nrm(ks[7], (L, POOL_WIDTH), 0.1),
        "w_out": nrm(ks[8], (L, D_MIX, D_MODEL), (2.0 * D_MIX) ** -0.5),
        "xa_norm_g": 1.0 + nrm(ks[9], (L, D_MODEL), 0.02),
        "mem_norm_g": 1.0 + nrm(ks[10], (D_MODEL,), 0.02),
        "xa_wq": nrm(ks[11], (L, D_MODEL, D_MODEL), D_MODEL ** -0.5),
        "xa_wkv": nrm(ks[12], (L, D_MODEL, 2 * D_MODEL), D_MODEL ** -0.5),
        "xa_wo": nrm(ks[13], (L, D_MODEL, D_MODEL), (2.0 * D_MODEL) ** -0.5),
        "ffn_norm_g": 1.0 + nrm(ks[14], (L, D_MODEL), 0.02),
        "ffn_w_up": nrm(ks[15], (L, D_MODEL, 2 * D_FF), D_MODEL ** -0.5),
        "ffn_conv_w": nrm(ks[16], (L, CONV_WIDTH, 2 * D_FF), CONV_WIDTH ** -0.5),
        "ffn_conv_b": nrm(ks[17], (L, 2 * D_FF), 0.01),
        "ffn_w_down": nrm(ks[18], (L, D_FF, D_MODEL), (2.0 * D_FF) ** -0.5),
        "final_norm_g": 1.0 + nrm(ks[19], (D_MODEL,), 0.02),
    }


def reference(x, mem, positions, mix_norm_g, w_in, ret_gn_g, pool_w, pool_scale, w_out,
              xa_norm_g, mem_norm_g, xa_wq, xa_wkv, xa_wo, ffn_norm_g, ffn_w_up,
              ffn_conv_w, ffn_conv_b, ffn_w_down, final_norm_g):
    mem_n = rmsnorm(mem, mem_norm_g)
    R = RET_WIDTH
    for l in range(DEPTH):
        h = rmsnorm(x, mix_norm_g[l])
        proj = h @ w_in[l]
        q, k, v, g, u = jnp.split(proj, [R, 2 * R, 3 * R, 4 * R], axis=-1)
        y_ret = jax.nn.silu(g) * head_rmsnorm(retention(q, k, v, positions), ret_gn_g[l])
        y_pool = pool_mixer(u, pool_w[l], pool_scale[l])
        x = x + jnp.concatenate([y_ret, y_pool], axis=-1) @ w_out[l]
        x = x + cross_attn(rmsnorm(x, xa_norm_g[l]), mem_n, xa_wq[l], xa_wkv[l], xa_wo[l])
        x = x + conv_glu_ffn(rmsnorm(x, ffn_norm_g[l]), ffn_w_up[l], ffn_conv_w[l],
                             ffn_conv_b[l], ffn_w_down[l])
    return rmsnorm(x, final_norm_g)
```

```python
import functools
import math

import jax
import jax.numpy as jnp
from jax import lax
from jax.experimental import pallas as pl
from jax.experimental.pallas import tpu as pltpu

F32 = jnp.float32
BF16 = jnp.bfloat16

D_MODEL = 1024
N_MEM = 256
RET_WIDTH = 512
RET_HEADS = 4
RET_HEAD_DIM = 128
POOL_WIDTH = 512
POOL_WINDOWS = (2, 4, 8, 16)
POOL_GROUP = 128
POOL_TAIL = 16
CHUNK = 128
ROPE_BASE = 10000.0
XA_HEADS = 4
XA_HEAD_DIM = 256
D_FF = 2816
FF_BLOCK = 256
N_FF_BLOCKS = D_FF // FF_BLOCK
CONV_TAIL = 8
EPS = 1e-6

SEQ_TILE = 512
VMEM_LIMIT_BYTES = 56 * 1024 * 1024


def _rmsnorm(x, g):
    ms = jnp.mean(x * x, axis=-1, keepdims=True)
    return x * lax.rsqrt(ms + EPS) * g


def _dot(a, b):
    return jnp.dot(a, b, preferred_element_type=F32)


def _dot_nt(a, b):
    return lax.dot_general(a, b, (((1,), (1,)), ((), ())), preferred_element_type=F32)


def _dot_tn(a, b):
    return lax.dot_general(a, b, (((0,), (0,)), ((), ())), preferred_element_type=F32)


def _const_spec(shape, index):
    return pl.BlockSpec(shape, lambda b, s: index, pipeline_mode=pl.Buffered(1))


def _row_spec(n, l):
    return _const_spec((None, 1, n), (l, 0, 0))


def _rows(a):
    return a.reshape(a.shape[0], 1, a.shape[1])


def _rope_kernel(pos_ref, cos_ref, sin_ref):
    pos = pos_ref[0]
    lane = lax.broadcasted_iota(jnp.int32, (1, RET_HEAD_DIM), 1)
    half = RET_HEAD_DIM // 2
    j2 = (2 * jnp.where(lane < half, lane, lane - half)).astype(F32)
    inv_freq = jnp.exp(-math.log(ROPE_BASE) * j2 / RET_HEAD_DIM)
    ang = pos * inv_freq
    sin = jnp.sin(ang)
    cos_ref[0] = jnp.cos(ang)
    sin_ref[0] = jnp.where(lane < half, -sin, sin)


def _rope_tables(positions):
    B, S = positions.shape
    T = SEQ_TILE
    pos = positions.astype(F32)[..., None]
    out = jax.ShapeDtypeStruct((B, S, RET_HEAD_DIM), F32)
    return pl.pallas_call(
        _rope_kernel,
        grid=(B, S // T),
        in_specs=[pl.BlockSpec((1, T, 1), lambda b, s: (b, s, 0))],
        out_specs=[pl.BlockSpec((1, T, RET_HEAD_DIM), lambda b, s: (b, s, 0))] * 2,
        out_shape=[out, out],
        name="rope_tables",
    )(pos)


def _kv_kernel(mem_ref, g_ref, wkv_ref, kt_ref, v_ref):
    mem_n = _rmsnorm(mem_ref[0], g_ref[...]).astype(BF16)
    k = _dot(mem_n, wkv_ref[:, :D_MODEL])
    kt_ref[0] = k.T.astype(BF16)
    v_ref[0] = _dot(mem_n, wkv_ref[:, D_MODEL:]).astype(BF16)


def _memory_kv(mem, mem_norm_g, wkv_b):
    L = wkv_b.shape[0]
    B = mem.shape[0]
    return pl.pallas_call(
        _kv_kernel,
        grid=(L, B),
        in_specs=[
            pl.BlockSpec((1, N_MEM, D_MODEL), lambda l, b: (b, 0, 0)),
            pl.BlockSpec((1, D_MODEL), lambda l, b: (0, 0)),
            pl.BlockSpec((None, D_MODEL, 2 * D_MODEL), lambda l, b: (l, 0, 0)),
        ],
        out_specs=[
            pl.BlockSpec((None, 1, D_MODEL, N_MEM), lambda l, b: (l, b, 0, 0)),
            pl.BlockSpec((None, 1, N_MEM, D_MODEL), lambda l, b: (l, b, 0, 0)),
        ],
        out_shape=[
            jax.ShapeDtypeStruct((L, B, D_MODEL, N_MEM), BF16),
            jax.ShapeDtypeStruct((L, B, N_MEM, D_MODEL), BF16),
        ],
        compiler_params=pltpu.CompilerParams(vmem_limit_bytes=VMEM_LIMIT_BYTES),
        name="memory_kv",
    )(mem, mem_norm_g.reshape(1, D_MODEL), wkv_b)


def _rope(t, cosf, sinf):
    return t * cosf + pltpu.roll(t, RET_HEAD_DIM // 2, 1) * sinf


def _mix_kernel(x_ref, cos_ref, sin_ref, ng_ref, win_ref, gng_ref, pw_ref, ps_ref, wout_ref,
                o_ref, state_ref, ubuf_ref, ybuf_ref, *, T):
    s = pl.program_id(1)

    @pl.when(s == 0)
    def _():
        state_ref[...] = jnp.zeros_like(state_ref)
        ubuf_ref[0:POOL_TAIL, :] = jnp.zeros((POOL_TAIL, POOL_WIDTH), F32)

    R = RET_WIDTH
    h = _rmsnorm(x_ref[0], ng_ref[...]).astype(BF16)
    q = _dot(h, win_ref[:, 0:R])
    k = _dot(h, win_ref[:, R:2 * R])
    v = _dot(h, win_ref[:, 2 * R:3 * R]).astype(BF16)
    g = _dot(h, win_ref[:, 3 * R:4 * R])
    ubuf_ref[POOL_TAIL:POOL_TAIL + T, :] = _dot(h, win_ref[:, 4 * R:4 * R + POOL_WIDTH])

    cosf = cos_ref[0]
    sinf = sin_ref[0]
    row = lax.broadcasted_iota(jnp.int32, (CHUNK, CHUNK), 0)
    col = lax.broadcasted_iota(jnp.int32, (CHUNK, CHUNK), 1)
    diff = (row - col).astype(F32)
    rowf = row.astype(F32)

    for hd in range(RET_HEADS):
        lanes = slice(hd * RET_HEAD_DIM, (hd + 1) * RET_HEAD_DIM)
        lg = math.log1p(-(2.0 ** (-5 - hd)))
        dmat = jnp.where(diff >= 0, jnp.exp(lg * jnp.maximum(diff, 0.0)), 0.0)
        zeta = jnp.exp(lg * (CHUNK - 1 - rowf))
        xi = jnp.exp(lg * (rowf + 1.0))
        chunk_decay = math.exp(lg * CHUNK)
        qr = _rope(q[:, lanes], cosf, sinf)
        kr = _rope(k[:, lanes], cosf, sinf) * (RET_HEAD_DIM ** -0.5)
        gn = gng_ref[:, lanes]
        state = state_ref[hd]
        for c in range(T // CHUNK):
            rows = slice(c * CHUNK, (c + 1) * CHUNK)
            qc = qr[rows].astype(BF16)
            kc = kr[rows]
            vc = v[rows, lanes]
            scores = _dot_nt(qc, kc.astype(BF16)) * dmat
            intra = _dot(scores.astype(BF16), vc)
            cross = _dot(qc, state.astype(BF16)) * xi
            kv = _dot_tn((kc * zeta).astype(BF16), vc)
            state = state * chunk_decay + kv
            ret = intra + cross
            ms = jnp.mean(ret * ret, axis=-1, keepdims=True)
            yn = ret * lax.rsqrt(ms + EPS) * gn
            gc = g[rows, lanes]
            ybuf_ref[rows, lanes] = (gc * jax.nn.sigmoid(gc) * yn).astype(BF16)
        state_ref[hd] = state

    t_abs = s * T + lax.broadcasted_iota(jnp.int32, (T, POOL_GROUP), 0)
    for gi, wlen in enumerate(POOL_WINDOWS):
        lanes = slice(gi * POOL_GROUP, (gi + 1) * POOL_GROUP)
        u = ubuf_ref[POOL_TAIL:POOL_TAIL + T, lanes]
        acc = u
        for back in range(1, wlen):
            acc = acc + ubuf_ref[POOL_TAIL - back:POOL_TAIL - back + T, lanes]
        inv_count = 1.0 / jnp.minimum(t_abs + 1, wlen).astype(F32)
        pooled = (acc * inv_count - u).astype(BF16)
        og = _dot(pooled, pw_ref[gi]) * ps_ref[:, lanes]
        ybuf_ref[:, RET_WIDTH + gi * POOL_GROUP:RET_WIDTH + (gi + 1) * POOL_GROUP] = og.astype(BF16)
    ubuf_ref[0:POOL_TAIL, :] = ubuf_ref[T:T + POOL_TAIL, :]

    o_ref[0] = x_ref[0] + _dot(ybuf_ref[...], wout_ref[...])


def _mixer(x, cosf, sinf, l, mix_norm_g, w_in_b, ret_gn_g, pool_w_b, pool_scale, w_out_b):
    B, S, D = x.shape
    T = SEQ_TILE
    tile = pl.BlockSpec((1, T, D), lambda b, s: (b, s, 0))
    rope = pl.BlockSpec((1, T, RET_HEAD_DIM), lambda b, s: (b, s, 0))
    return pl.pallas_call(
        functools.partial(_mix_kernel, T=T),
        grid=(B, S // T),
        in_specs=[
            tile, rope, rope,
            _row_spec(D, l),
            _const_spec((None, D, w_in_b.shape[-1]), (l, 0, 0)),
            _row_spec(RET_WIDTH, l),
            _const_spec((None, len(POOL_WINDOWS), POOL_GROUP, POOL_GROUP), (l, 0, 0, 0)),
            _row_spec(POOL_WIDTH, l),
            _const_spec((None, D, D), (l, 0, 0)),
        ],
        out_specs=tile,
        out_shape=jax.ShapeDtypeStruct(x.shape, F32),
        scratch_shapes=[
            pltpu.VMEM((RET_HEADS, RET_HEAD_DIM, RET_HEAD_DIM), F32),
            pltpu.VMEM((T + POOL_TAIL, POOL_WIDTH), F32),
            pltpu.VMEM((T, D), BF16),
        ],
        compiler_params=pltpu.CompilerParams(
            dimension_semantics=("parallel", "arbitrary"),
            vmem_limit_bytes=VMEM_LIMIT_BYTES),
        name=f"mixer_l{l}",
    )(x, cosf, sinf, mix_norm_g, w_in_b, ret_gn_g, pool_w_b, pool_scale, w_out_b)


def _xa_kernel(x_ref, ng_ref, wq_ref, kt_ref, v_ref, wo_ref, o_ref, obuf_ref):
    h = _rmsnorm(x_ref[0], ng_ref[...]).astype(BF16)
    q = _dot(h, wq_ref[...])
    for hd in range(XA_HEADS):
        dims = slice(hd * XA_HEAD_DIM, (hd + 1) * XA_HEAD_DIM)
        sc = _dot(q[:, dims].astype(BF16), kt_ref[0, dims, :]) * (XA_HEAD_DIM ** -0.5)
        p = jnp.exp(sc - jnp.max(sc, axis=-1, keepdims=True))
        p = p * (1.0 / jnp.sum(p, axis=-1, keepdims=True))
        obuf_ref[:, dims] = _dot(p.astype(BF16), v_ref[0, :, dims]).astype(BF16)
    o_ref[0] = x_ref[0] + _dot(obuf_ref[...], wo_ref[...])


def _cross_attention(x, l, xa_norm_g, wq_b, kt, v, wo_b):
    B, S, D = x.shape
    T = SEQ_TILE
    tile = pl.BlockSpec((1, T, D), lambda b, s: (b, s, 0))
    return pl.pallas_call(
        _xa_kernel,
        grid=(B, S // T),
        in_specs=[
            tile,
            _row_spec(D, l),
            _const_spec((None, D, D), (l, 0, 0)),
            pl.BlockSpec((None, 1, D, N_MEM), lambda b, s: (l, b, 0, 0)),
            pl.BlockSpec((None, 1, N_MEM, D), lambda b, s: (l, b, 0, 0)),
            _const_spec((None, D, D), (l, 0, 0)),
        ],
        out_specs=tile,
        out_shape=jax.ShapeDtypeStruct(x.shape, F32),
        scratch_shapes=[pltpu.VMEM((T, D), BF16)],
        compiler_params=pltpu.CompilerParams(
            dimension_semantics=("parallel", "arbitrary"),
            vmem_limit_bytes=VMEM_LIMIT_BYTES),
        name=f"cross_attn_l{l}",
    )(x, xa_norm_g, wq_b, kt, v, wo_b)


def _causal_conv(a, abuf_ref, tail_ref, cw, T):
    abuf_ref[0:CONV_TAIL, :] = tail_ref[...]
    abuf_ref[CONV_TAIL:CONV_TAIL + T, :] = a
    a1 = abuf_ref[CONV_TAIL - 1:CONV_TAIL - 1 + T, :]
    a2 = abuf_ref[CONV_TAIL - 2:CONV_TAIL - 2 + T, :]
    tail_ref[...] = abuf_ref[T:T + CONV_TAIL, :]
    return cw[0:1] * a2 + cw[1:2] * a1 + cw[2:3] * a + cw[3:4]


def _ffn_kernel(x_ref, ng_ref, wg_ref, wv_ref, cw_ref, wd_ref, fg_ref, o_ref,
                hbuf_ref, abuf_g_ref, abuf_v_ref, tail_ref, *, T, final):
    @pl.when(pl.program_id(1) == 0)
    def _():
        tail_ref[...] = jnp.zeros_like(tail_ref)

    hbuf_ref[...] = _rmsnorm(x_ref[0], ng_ref[...]).astype(BF16)
    o_ref[0] = x_ref[0]

    def block(j, carry):
        h = hbuf_ref[...]
        cg = _causal_conv(_dot(h, wg_ref[j]), abuf_g_ref, tail_ref.at[j], cw_ref[j], T)
        cv = _causal_conv(_dot(h, wv_ref[j]), abuf_v_ref, tail_ref.at[N_FF_BLOCKS + j],
                          cw_ref[N_FF_BLOCKS + j], T)
        act = (cg * jax.nn.sigmoid(cg) * cv).astype(BF16)
        o_ref[0] += _dot(act, wd_ref[j])
        return carry

    lax.fori_loop(0, N_FF_BLOCKS, block, 0)
    if final:
        o_ref[0] = _rmsnorm(o_ref[0], fg_ref[...])


def _ffn(x, l, ffn_norm_g, wg_b, wv_b, conv_pack, wd_b, final_g, final):
    B, S, D = x.shape
    T = SEQ_TILE
    NB = N_FF_BLOCKS
    tile = pl.BlockSpec((1, T, D), lambda b, s: (b, s, 0))
    return pl.pallas_call(
        functools.partial(_ffn_kernel, T=T, final=final),
        grid=(B, S // T),
        in_specs=[
            tile,
            _row_spec(D, l),
            _const_spec((None, NB, D, FF_BLOCK), (l, 0, 0, 0)),
            _const_spec((None, NB, D, FF_BLOCK), (l, 0, 0, 0)),
            _const_spec((None, 2 * NB, CONV_TAIL, FF_BLOCK), (l, 0, 0, 0)),
            _const_spec((None, NB, FF_BLOCK, D), (l, 0, 0, 0)),
            _const_spec((1, D), (0, 0)),
        ],
        out_specs=tile,
        out_shape=jax.ShapeDtypeStruct(x.shape, F32),
        scratch_shapes=[
            pltpu.VMEM((T, D), BF16),
            pltpu.VMEM((T + CONV_TAIL, FF_BLOCK), F32),
            pltpu.VMEM((T + CONV_TAIL, FF_BLOCK), F32),
            pltpu.VMEM((2 * NB, CONV_TAIL, FF_BLOCK), F32),
        ],
        compiler_params=pltpu.CompilerParams(
            dimension_semantics=("parallel", "arbitrary"),
            vmem_limit_bytes=VMEM_LIMIT_BYTES),
        name=f"ffn_l{l}",
    )(x, ffn_norm_g, wg_b, wv_b, conv_pack, wd_b, final_g)


def _ffn_weights(ffn_w_up, ffn_conv_w, ffn_conv_b, ffn_w_down):
    L = ffn_w_up.shape[0]
    NB = N_FF_BLOCKS
    up = ffn_w_up.astype(BF16).reshape(L, D_MODEL, 2, NB, FF_BLOCK)
    wg = up[:, :, 0].transpose(0, 2, 1, 3)
    wv = up[:, :, 1].transpose(0, 2, 1, 3)
    taps = jnp.concatenate([ffn_conv_w, ffn_conv_b[:, None, :]], axis=1)
    taps = jnp.pad(taps, ((0, 0), (0, CONV_TAIL - taps.shape[1]), (0, 0)))
    conv_pack = taps.reshape(L, CONV_TAIL, 2 * NB, FF_BLOCK).transpose(0, 2, 1, 3)
    wd = ffn_w_down.astype(BF16).reshape(L, NB, FF_BLOCK, D_MODEL)
    return wg, wv, conv_pack, wd


def kernel(x, mem, positions, mix_norm_g, w_in, ret_gn_g, pool_w, pool_scale, w_out, xa_norm_g,
           mem_norm_g, xa_wq, xa_wkv, xa_wo, ffn_norm_g, ffn_w_up, ffn_conv_w, ffn_conv_b,
           ffn_w_down, final_norm_g):
    L = w_in.shape[0]
    assert x.shape[1] % SEQ_TILE == 0 and SEQ_TILE % CHUNK == 0
    w_in_b, w_out_b, pool_w_b = w_in.astype(BF16), w_out.astype(BF16), pool_w.astype(BF16)
    wq_b, wkv_b, wo_b = xa_wq.astype(BF16), xa_wkv.astype(BF16), xa_wo.astype(BF16)
    wg_b, wv_b, conv_pack, wd_b = _ffn_weights(ffn_w_up, ffn_conv_w, ffn_conv_b, ffn_w_down)
    final_g = final_norm_g.reshape(1, D_MODEL)
    mix_norm_g, ret_gn_g, pool_scale = _rows(mix_norm_g), _rows(ret_gn_g), _rows(pool_scale)
    xa_norm_g, ffn_norm_g = _rows(xa_norm_g), _rows(ffn_norm_g)

    cosf, sinf = _rope_tables(positions)
    kt, v = _memory_kv(mem, mem_norm_g, wkv_b)
    for l in range(L):
        x = _mixer(x, cosf, sinf, l, mix_norm_g, w_in_b, ret_gn_g, pool_w_b, pool_scale, w_out_b)
        x = _cross_attention(x, l, xa_norm_g, wq_b, kt, v, wo_b)
        x = _ffn(x, l, ffn_norm_g, wg_b, wv_b, conv_pack, wd_b, final_g, final=(l == L - 1))
    return x
```

```python
import functools
import math

import jax
import jax.numpy as jnp
from jax import lax
from jax.experimental import pallas as pl
from jax.experimental.pallas import tpu as pltpu

F32 = jnp.float32
BF16 = jnp.bfloat16

D_MODEL = 1024
N_MEM = 256
RET_WIDTH = 512
RET_HEADS = 4
RET_HEAD_DIM = 128
POOL_WIDTH = 512
POOL_WINDOWS = (2, 4, 8, 16)
POOL_GROUP = 128
POOL_TAIL = 16
CHUNK = 128
ROPE_BASE = 10000.0
XA_HEADS = 4
XA_HEAD_DIM = 256
D_FF = 2816
FF_BLOCK = 256
N_FF_BLOCKS = D_FF // FF_BLOCK
CONV_TAIL = 8
DOWN_GROUP = 11
EPS = 1e-6

ROW_BLOCK = 512
MIX_TILE = 1024
XA_TILE = 1024
FFN_TILE = 512
VMEM_LIMIT_BYTES = 56 * 1024 * 1024


def _rmsnorm(x, g):
    ms = jnp.mean(x * x, axis=-1, keepdims=True)
    return x * lax.rsqrt(ms + EPS) * g


def _dot(a, b):
    return jnp.dot(a, b, preferred_element_type=F32)


def _dot_nt(a, b):
    return lax.dot_general(a, b, (((1,), (1,)), ((), ())), preferred_element_type=F32)


def _dot_tn(a, b):
    return lax.dot_general(a, b, (((0,), (0,)), ((), ())), preferred_element_type=F32)


def _const_spec(shape, index):
    return pl.BlockSpec(shape, lambda b, s: index, pipeline_mode=pl.Buffered(1))


def _row_spec(n, l):
    return _const_spec((None, 1, n), (l, 0, 0))


def _rows(a):
    return a.reshape(a.shape[0], 1, a.shape[1])


def _rope_kernel(pos_ref, cos_ref, sin_ref):
    pos = pos_ref[0]
    lane = lax.broadcasted_iota(jnp.int32, (1, RET_HEAD_DIM), 1)
    half = RET_HEAD_DIM // 2
    j2 = (2 * jnp.where(lane < half, lane, lane - half)).astype(F32)
    inv_freq = jnp.exp(-math.log(ROPE_BASE) * j2 / RET_HEAD_DIM)
    ang = pos * inv_freq
    sin = jnp.sin(ang)
    cos_ref[0] = jnp.cos(ang)
    sin_ref[0] = jnp.where(lane < half, -sin, sin)


def _rope_tables(positions):
    B, S = positions.shape
    T = ROW_BLOCK
    pos = positions.astype(F32)[..., None]
    out = jax.ShapeDtypeStruct((B, S, RET_HEAD_DIM), F32)
    return pl.pallas_call(
        _rope_kernel,
        grid=(B, S // T),
        in_specs=[pl.BlockSpec((1, T, 1), lambda b, s: (b, s, 0))],
        out_specs=[pl.BlockSpec((1, T, RET_HEAD_DIM), lambda b, s: (b, s, 0))] * 2,
        out_shape=[out, out],
        name="rope_tables",
    )(pos)


def _kv_kernel(mem_ref, g_ref, wkv_ref, kt_ref, v_ref):
    mem_n = _rmsnorm(mem_ref[0], g_ref[...]).astype(BF16)
    k = _dot(mem_n, wkv_ref[:, :D_MODEL])
    kt_ref[0] = k.T.astype(BF16)
    v_ref[0] = _dot(mem_n, wkv_ref[:, D_MODEL:]).astype(BF16)


def _memory_kv(mem, mem_norm_g, wkv_b):
    L = wkv_b.shape[0]
    B = mem.shape[0]
    return pl.pallas_call(
        _kv_kernel,
        grid=(L, B),
        in_specs=[
            pl.BlockSpec((1, N_MEM, D_MODEL), lambda l, b: (b, 0, 0)),
            pl.BlockSpec((1, D_MODEL), lambda l, b: (0, 0)),
            pl.BlockSpec((None, D_MODEL, 2 * D_MODEL), lambda l, b: (l, 0, 0)),
        ],
        out_specs=[
            pl.BlockSpec((None, 1, D_MODEL, N_MEM), lambda l, b: (l, b, 0, 0)),
            pl.BlockSpec((None, 1, N_MEM, D_MODEL), lambda l, b: (l, b, 0, 0)),
        ],
        out_shape=[
            jax.ShapeDtypeStruct((L, B, D_MODEL, N_MEM), BF16),
            jax.ShapeDtypeStruct((L, B, N_MEM, D_MODEL), BF16),
        ],
        compiler_params=pltpu.CompilerParams(vmem_limit_bytes=VMEM_LIMIT_BYTES),
        name="memory_kv",
    )(mem, mem_norm_g.reshape(1, D_MODEL), wkv_b)


def _rope(t, cosf, sinf):
    return t * cosf + pltpu.roll(t, RET_HEAD_DIM // 2, 1) * sinf


def _trailing_window_sum(ext, wlen):
    span = 1
    while span < wlen:
        ext = ext + pltpu.roll(ext, span, 0)
        span *= 2
    return ext


def _mix_kernel(x_ref, cos_ref, sin_ref, ng_ref, win_ref, gng_ref, pw_ref, ps_ref, wout_ref,
                o_ref, state_ref, ubuf_ref, ybuf_ref, *, T):
    s = pl.program_id(1)

    @pl.when(s == 0)
    def _():
        state_ref[...] = jnp.zeros_like(state_ref)
        ubuf_ref[0:POOL_TAIL, :] = jnp.zeros((POOL_TAIL, POOL_WIDTH), F32)

    R = RET_WIDTH
    row = lax.broadcasted_iota(jnp.int32, (CHUNK, CHUNK), 0)
    col = lax.broadcasted_iota(jnp.int32, (CHUNK, CHUNK), 1)
    diff = (row - col).astype(F32)
    rowf = row.astype(F32)
    k_scale = RET_HEAD_DIM ** -0.5
    decay_consts = []
    for hd in range(RET_HEADS):
        lg = math.log1p(-(2.0 ** (-5 - hd)))
        dmat = jnp.where(diff >= 0, jnp.exp(lg * jnp.maximum(diff, 0.0)), 0.0) * k_scale
        zeta = jnp.exp(lg * (CHUNK - 1 - rowf)) * k_scale
        xi = jnp.exp(lg * (rowf + 1.0))
        decay_consts.append((dmat, zeta, xi, math.exp(lg * CHUNK)))

    n_blocks = T // ROW_BLOCK
    n_chunks = ROW_BLOCK // CHUNK
    head_lanes = [slice(hd * RET_HEAD_DIM, (hd + 1) * RET_HEAD_DIM) for hd in range(RET_HEADS)]
    st = [dict() for _ in range(n_blocks)]

    def rows_of(i):
        return slice(i * ROW_BLOCK, (i + 1) * ROW_BLOCK)

    def chunk_rows(c):
        return slice(c * CHUNK, (c + 1) * CHUNK)

    def project_qk(i):
        b = st[i]
        b["h"] = _rmsnorm(x_ref[0, rows_of(i)], ng_ref[...]).astype(BF16)
        b["q"] = _dot(b["h"], win_ref[:, 0:R])
        b["k"] = _dot(b["h"], win_ref[:, R:2 * R])

    def project_rest(i):
        b = st[i]
        b["v"] = _dot(b["h"], win_ref[:, 2 * R:3 * R]).astype(BF16)
        b["g"] = _dot(b["h"], win_ref[:, 3 * R:4 * R])
        r0 = i * ROW_BLOCK
        ubuf_ref[POOL_TAIL + r0:POOL_TAIL + r0 + ROW_BLOCK, :] = _dot(
            b["h"], win_ref[:, 4 * R:4 * R + POOL_WIDTH])

    def retention_scores(i):
        b = st[i]
        cosf = cos_ref[0, rows_of(i)]
        sinf = sin_ref[0, rows_of(i)]
        b["qc"], b["scores"], b["kv"] = {}, {}, {}
        for hd, lanes in enumerate(head_lanes):
            zeta = decay_consts[hd][1]
            qr = _rope(b["q"][:, lanes], cosf, sinf).astype(BF16)
            kr = _rope(b["k"][:, lanes], cosf, sinf)
            for c in range(n_chunks):
                rows = chunk_rows(c)
                vc = b["v"][rows, lanes]
                b["qc"][hd, c] = qr[rows]
                b["scores"][hd, c] = _dot_nt(qr[rows], kr[rows].astype(BF16))
                b["kv"][hd, c] = _dot_tn((kr[rows] * zeta).astype(BF16), vc)

    def retention_states(i):
        b = st[i]
        b["p"], b["state_in"] = {}, {}
        for hd in range(RET_HEADS):
            dmat, _, _, chunk_decay = decay_consts[hd]
            state = state_ref[hd]
            for c in range(n_chunks):
                b["p"][hd, c] = (b["scores"][hd, c] * dmat).astype(BF16)
                b["state_in"][hd, c] = state.astype(BF16)
                state = state * chunk_decay + b["kv"][hd, c]
            state_ref[hd] = state

    def retention_outputs(i):
        b = st[i]
        r0 = i * ROW_BLOCK
        for hd, lanes in enumerate(head_lanes):
            xi = decay_consts[hd][2]
            gn = gng_ref[:, lanes]
            for c in range(n_chunks):
                rows = chunk_rows(c)
                intra = _dot(b["p"][hd, c], b["v"][rows, lanes])
                cross = _dot(b["qc"][hd, c], b["state_in"][hd, c]) * xi
                ret = intra + cross
                ms = jnp.mean(ret * ret, axis=-1, keepdims=True)
                yn = ret * lax.rsqrt(ms + EPS) * gn
                gc = b["g"][rows, lanes]
                ybuf_ref[r0 + c * CHUNK:r0 + (c + 1) * CHUNK, lanes] = (
                    gc * jax.nn.sigmoid(gc) * yn).astype(BF16)

    def pooling(i):
        r0 = i * ROW_BLOCK
        t_next = (s * T + r0 + 1 + lax.broadcasted_iota(jnp.int32, (ROW_BLOCK, POOL_GROUP), 0)
                  ).astype(F32)
        for gi, wlen in enumerate(POOL_WINDOWS):
            lanes = slice(gi * POOL_GROUP, (gi + 1) * POOL_GROUP)
            ext = ubuf_ref[r0:r0 + ROW_BLOCK + POOL_TAIL, lanes]
            wsum = _trailing_window_sum(ext, wlen)[POOL_TAIL:]
            inv_count = 1.0 / jnp.minimum(t_next, float(wlen))
            pooled = (wsum * inv_count - ext[POOL_TAIL:]).astype(BF16)
            og = _dot(pooled, pw_ref[gi]) * ps_ref[:, lanes]
            ybuf_ref[rows_of(i), R + gi * POOL_GROUP:R + (gi + 1) * POOL_GROUP] = og.astype(BF16)

    def output_pool_half(i):
        o_ref[0, rows_of(i)] = x_ref[0, rows_of(i)] + _dot(ybuf_ref[rows_of(i), R:], wout_ref[R:, :])

    def output_retention_half(i):
        o_ref[0, rows_of(i)] += _dot(ybuf_ref[rows_of(i), :R], wout_ref[:R, :])

    project_qk(0)
    project_rest(0)
    retention_scores(0)
    for i in range(n_blocks):
        has_next = i + 1 < n_blocks
        if has_next:
            project_qk(i + 1)
        retention_states(i)
        retention_outputs(i)
        pooling(i)
        if has_next:
            project_rest(i + 1)
            retention_scores(i + 1)
        if i > 0:
            output_retention_half(i - 1)
        output_pool_half(i)
    output_retention_half(n_blocks - 1)

    ubuf_ref[0:POOL_TAIL, :] = ubuf_ref[T:T + POOL_TAIL, :]


def _mixer(x, cosf, sinf, l, mix_norm_g, w_in_b, ret_gn_g, pool_w_b, pool_scale, w_out_b):
    B, S, D = x.shape
    T = MIX_TILE
    tile = pl.BlockSpec((1, T, D), lambda b, s: (b, s, 0))
    rope = pl.BlockSpec((1, T, RET_HEAD_DIM), lambda b, s: (b, s, 0))
    return pl.pallas_call(
        functools.partial(_mix_kernel, T=T),
        grid=(B, S // T),
        in_specs=[
            tile, rope, rope,
            _row_spec(D, l),
            _const_spec((None, D, w_in_b.shape[-1]), (l, 0, 0)),
            _row_spec(RET_WIDTH, l),
            _const_spec((None, len(POOL_WINDOWS), POOL_GROUP, POOL_GROUP), (l, 0, 0, 0)),
            _row_spec(POOL_WIDTH, l),
            _const_spec((None, D, D), (l, 0, 0)),
        ],
        out_specs=tile,
        out_shape=jax.ShapeDtypeStruct(x.shape, F32),
        scratch_shapes=[
            pltpu.VMEM((RET_HEADS, RET_HEAD_DIM, RET_HEAD_DIM), F32),
            pltpu.VMEM((T + POOL_TAIL, POOL_WIDTH), F32),
            pltpu.VMEM((T, D), BF16),
        ],
        compiler_params=pltpu.CompilerParams(
            dimension_semantics=("parallel", "arbitrary"),
            vmem_limit_bytes=VMEM_LIMIT_BYTES),
        name=f"mixer_l{l}",
    )(x, cosf, sinf, mix_norm_g, w_in_b, ret_gn_g, pool_w_b, pool_scale, w_out_b)


def _xa_kernel(x_ref, ng_ref, wq_ref, kt_ref, v_ref, wo_ref, o_ref, obuf_ref, *, T):
    n_blocks = T // ROW_BLOCK
    heads = [slice(hd * XA_HEAD_DIM, (hd + 1) * XA_HEAD_DIM) for hd in range(XA_HEADS)]
    scores = [None] * n_blocks

    def rows_of(i):
        return slice(i * ROW_BLOCK, (i + 1) * ROW_BLOCK)

    def score(i):
        h = _rmsnorm(x_ref[0, rows_of(i)], ng_ref[...]).astype(BF16)
        q = _dot(h, wq_ref[...])
        scores[i] = [_dot(q[:, dims].astype(BF16), kt_ref[0, dims, :]) * (XA_HEAD_DIM ** -0.5)
                     for dims in heads]

    def attend(i):
        for dims, sc in zip(heads, scores[i]):
            p = jnp.exp(sc - jnp.max(sc, axis=-1, keepdims=True))
            p = p * (1.0 / jnp.sum(p, axis=-1, keepdims=True))
            obuf_ref[rows_of(i), dims] = _dot(p.astype(BF16), v_ref[0, :, dims]).astype(BF16)

    def project_out(i):
        o_ref[0, rows_of(i)] = x_ref[0, rows_of(i)] + _dot(obuf_ref[rows_of(i), :], wo_ref[...])

    score(0)
    for i in range(n_blocks):
        if i + 1 < n_blocks:
            score(i + 1)
        attend(i)
        if i > 0:
            project_out(i - 1)
    project_out(n_blocks - 1)


def _cross_attention(x, l, xa_norm_g, wq_b, kt, v, wo_b):
    B, S, D = x.shape
    T = XA_TILE
    tile = pl.BlockSpec((1, T, D), lambda b, s: (b, s, 0))
    return pl.pallas_call(
        functools.partial(_xa_kernel, T=T),
        grid=(B, S // T),
        in_specs=[
            tile,
            _row_spec(D, l),
            _const_spec((None, D, D), (l, 0, 0)),
            pl.BlockSpec((None, 1, D, N_MEM), lambda b, s: (l, b, 0, 0)),
            pl.BlockSpec((None, 1, N_MEM, D), lambda b, s: (l, b, 0, 0)),
            _const_spec((None, D, D), (l, 0, 0)),
        ],
        out_specs=tile,
        out_shape=jax.ShapeDtypeStruct(x.shape, F32),
        scratch_shapes=[pltpu.VMEM((T, D), BF16)],
        compiler_params=pltpu.CompilerParams(
            dimension_semantics=("parallel", "arbitrary"),
            vmem_limit_bytes=VMEM_LIMIT_BYTES),
        name=f"cross_attn_l{l}",
    )(x, xa_norm_g, wq_b, kt, v, wo_b)


def _shift_rows(a, prev, k):
    rolled = pltpu.roll(a, k, 0)
    row = lax.broadcasted_iota(jnp.int32, prev.shape, 0)
    head = jnp.where(row < k, pltpu.roll(prev, k, 0), rolled[0:CONV_TAIL])
    return jnp.concatenate([head, rolled[CONV_TAIL:]], axis=0)


def _causal_conv(a, tail_ref, cw):
    prev = tail_ref[...]
    tail_ref[...] = a[a.shape[0] - CONV_TAIL:]
    return (cw[0:1] * _shift_rows(a, prev, 2) + cw[1:2] * _shift_rows(a, prev, 1)
            + cw[2:3] * a + cw[3:4])


def _ffn_kernel(x_ref, ng_ref, wg_ref, wv_ref, cw_ref, wd_ref, fg_ref, o_ref,
                hbuf_ref, act_ref, tail_ref, *, final):
    @pl.when(pl.program_id(1) == 0)
    def _():
        tail_ref[...] = jnp.zeros_like(tail_ref)

    hbuf_ref[...] = _rmsnorm(x_ref[0], ng_ref[...]).astype(BF16)

    def up_block(j):
        h = hbuf_ref[...]
        cg = _causal_conv(_dot(h, wg_ref[j]), tail_ref.at[j], cw_ref[j])
        cv = _causal_conv(_dot(h, wv_ref[j]), tail_ref.at[N_FF_BLOCKS + j],
                          cw_ref[N_FF_BLOCKS + j])
        act_ref[:, j * FF_BLOCK:(j + 1) * FF_BLOCK] = (cg * jax.nn.sigmoid(cg) * cv).astype(BF16)

    def down_group(gidx):
        cols = slice(gidx * DOWN_GROUP * FF_BLOCK,
                     min((gidx + 1) * DOWN_GROUP, N_FF_BLOCKS) * FF_BLOCK)
        down = _dot(act_ref[:, cols], wd_ref[cols, :])
        if gidx == 0:
            o_ref[0] = x_ref[0] + down
        else:
            o_ref[0] += down

    n_groups = -(-N_FF_BLOCKS // DOWN_GROUP)
    emitted = 0
    for j in range(N_FF_BLOCKS):
        up_block(j)
        while (emitted + 1) * DOWN_GROUP <= j:
            down_group(emitted)
            emitted += 1
    while emitted < n_groups:
        down_group(emitted)
        emitted += 1
    if final:
        o_ref[0] = _rmsnorm(o_ref[0], fg_ref[...])


def _ffn(x, l, ffn_norm_g, wg_b, wv_b, conv_pack, wd_b, final_g, final):
    B, S, D = x.shape
    T = FFN_TILE
    NB = N_FF_BLOCKS
    tile = pl.BlockSpec((1, T, D), lambda b, s: (b, s, 0))
    return pl.pallas_call(
        functools.partial(_ffn_kernel, final=final),
        grid=(B, S // T),
        in_specs=[
            tile,
            _row_spec(D, l),
            _const_spec((None, NB, D, FF_BLOCK), (l, 0, 0, 0)),
            _const_spec((None, NB, D, FF_BLOCK), (l, 0, 0, 0)),
            _const_spec((None, 2 * NB, CONV_TAIL, FF_BLOCK), (l, 0, 0, 0)),
            _const_spec((None, D_FF, D), (l, 0, 0)),
            _const_spec((1, D), (0, 0)),
        ],
        out_specs=tile,
        out_shape=jax.ShapeDtypeStruct(x.shape, F32),
        scratch_shapes=[
            pltpu.VMEM((T, D), BF16),
            pltpu.VMEM((T, D_FF), BF16),
            pltpu.VMEM((2 * NB, CONV_TAIL, FF_BLOCK), F32),
        ],
        compiler_params=pltpu.CompilerParams(
            dimension_semantics=("parallel", "arbitrary"),
            vmem_limit_bytes=VMEM_LIMIT_BYTES),
        name=f"ffn_l{l}",
    )(x, ffn_norm_g, wg_b, wv_b, conv_pack, wd_b, final_g)


def _ffn_weights(ffn_w_up, ffn_conv_w, ffn_conv_b, ffn_w_down):
    L = ffn_w_up.shape[0]
    NB = N_FF_BLOCKS
    up = ffn_w_up.astype(BF16).reshape(L, D_MODEL, 2, NB, FF_BLOCK)
    wg = up[:, :, 0].transpose(0, 2, 1, 3)
    wv = up[:, :, 1].transpose(0, 2, 1, 3)
    taps = jnp.concatenate([ffn_conv_w, ffn_conv_b[:, None, :]], axis=1)
    taps = jnp.pad(taps, ((0, 0), (0, CONV_TAIL - taps.shape[1]), (0, 0)))
    conv_pack = taps.reshape(L, CONV_TAIL, 2 * NB, FF_BLOCK).transpose(0, 2, 1, 3)
    return wg, wv, conv_pack, ffn_w_down.astype(BF16)


def kernel(x, mem, positions, mix_norm_g, w_in, ret_gn_g, pool_w, pool_scale, w_out, xa_norm_g,
           mem_norm_g, xa_wq, xa_wkv, xa_wo, ffn_norm_g, ffn_w_up, ffn_conv_w, ffn_conv_b,
           ffn_w_down, final_norm_g):
    L = w_in.shape[0]
    S = x.shape[1]
    assert S % MIX_TILE == 0 and S % XA_TILE == 0 and S % FFN_TILE == 0
    assert MIX_TILE % ROW_BLOCK == 0 and XA_TILE % ROW_BLOCK == 0 and ROW_BLOCK % CHUNK == 0
    w_in_b, w_out_b, pool_w_b = w_in.astype(BF16), w_out.astype(BF16), pool_w.astype(BF16)
    wq_b, wkv_b, wo_b = xa_wq.astype(BF16), xa_wkv.astype(BF16), xa_wo.astype(BF16)
    wg_b, wv_b, conv_pack, wd_b = _ffn_weights(ffn_w_up, ffn_conv_w, ffn_conv_b, ffn_w_down)
    final_g = final_norm_g.reshape(1, D_MODEL)
    mix_norm_g, ret_gn_g, pool_scale = _rows(mix_norm_g), _rows(ret_gn_g), _rows(pool_scale)
    xa_norm_g, ffn_norm_g = _rows(xa_norm_g), _rows(ffn_norm_g)

    cosf, sinf = _rope_tables(positions)
    kt, v = _memory_kv(mem, mem_norm_g, wkv_b)
    for l in range(L):
        x = _mixer(x, cosf, sinf, l, mix_norm_g, w_in_b, ret_gn_g, pool_w_b, pool_scale, w_out_b)
        x = _cross_attention(x, l, xa_norm_g, wq_b, kt, v, wo_b)
        x = _ffn(x, l, ffn_norm_g, wg_b, wv_b, conv_pack, wd_b, final_g, final=(l == L - 1))
    return x
```

```python
import functools
import math

import jax
import jax.numpy as jnp
from jax import lax
from jax.experimental import pallas as pl
from jax.experimental.pallas import tpu as pltpu

F32 = jnp.float32
BF16 = jnp.bfloat16

D_MODEL = 1024
N_MEM = 256
RET_WIDTH = 512
RET_HEADS = 4
RET_HEAD_DIM = 128
POOL_WIDTH = 512
POOL_WINDOWS = (2, 4, 8, 16)
POOL_GROUP = 128
POOL_TAIL = 16
CHUNK = 128
ROPE_BASE = 10000.0
XA_HEADS = 4
XA_HEAD_DIM = 256
D_FF = 2816
FF_BLOCK = 128
N_FF_BLOCKS = D_FF // FF_BLOCK
CONV_TAIL = 8
DOWN_COLS = 256
KV_BATCHES = 4
EPS = 1e-6

ROW_BLOCK = 512
MIX_TILE = 1024
XA_TILE = 1024
FFN_TILE = 1024
VMEM_LIMIT_BYTES = 56 * 1024 * 1024


def _rmsnorm(x, g):
    ms = jnp.mean(x * x, axis=-1, keepdims=True)
    return x * lax.rsqrt(ms + EPS) * g


def _dot(a, b):
    return jnp.dot(a, b, preferred_element_type=F32)


def _dot_nt(a, b):
    return lax.dot_general(a, b, (((1,), (1,)), ((), ())), preferred_element_type=F32)


def _dot_tn(a, b):
    return lax.dot_general(a, b, (((0,), (0,)), ((), ())), preferred_element_type=F32)


def _const_spec(shape, index):
    return pl.BlockSpec(shape, lambda b, s: index, pipeline_mode=pl.Buffered(1))


def _row_spec(n, l):
    return _const_spec((None, 1, n), (l, 0, 0))


def _rows(a):
    return a.reshape(a.shape[0], 1, a.shape[1])


def _rope_kernel(pos_ref, cos_ref, sin_ref):
    T = pos_ref.shape[1]
    lane = lax.broadcasted_iota(jnp.int32, (1, RET_HEAD_DIM), 1)
    half = RET_HEAD_DIM // 2
    low = lane < half
    j2 = (2 * jnp.where(low, lane, lane - half)).astype(F32)
    inv_freq = jnp.exp(-math.log(ROPE_BASE) * j2 / RET_HEAD_DIM)
    pos = jnp.where(low, pos_ref[0, 0:T // 2], pos_ref[0, T // 2:T])
    ang = pos * inv_freq
    cos = jnp.cos(ang)
    sin = jnp.sin(ang)
    cos_sw = pltpu.roll(cos, half, 1)
    sin_sw = pltpu.roll(sin, half, 1)
    cos_ref[0, 0:T // 2] = jnp.where(low, cos, cos_sw)
    cos_ref[0, T // 2:T] = jnp.where(low, cos_sw, cos)
    sin_ref[0, 0:T // 2] = jnp.where(low, -sin, sin_sw)
    sin_ref[0, T // 2:T] = jnp.where(low, -sin_sw, sin)


def _rope_tables(positions):
    B, S = positions.shape
    T = ROW_BLOCK
    pos = positions.astype(F32)[..., None]
    out = jax.ShapeDtypeStruct((B, S, RET_HEAD_DIM), F32)
    return pl.pallas_call(
        _rope_kernel,
        grid=(B, S // T),
        in_specs=[pl.BlockSpec((1, T, 1), lambda b, s: (b, s, 0))],
        out_specs=[pl.BlockSpec((1, T, RET_HEAD_DIM), lambda b, s: (b, s, 0))] * 2,
        out_shape=[out, out],
        name="rope_tables",
    )(pos)


def _kv_kernel(mem_ref, g_ref, wkv_ref, kt_ref, v_ref):
    mem_n = _rmsnorm(mem_ref[...], g_ref[...]).astype(BF16)
    k = _dot(mem_n, wkv_ref[:, :D_MODEL])
    v = _dot(mem_n, wkv_ref[:, D_MODEL:]).astype(BF16)
    for bb in range(KV_BATCHES):
        rows = slice(bb * N_MEM, (bb + 1) * N_MEM)
        kt_ref[bb] = k[rows].T.astype(BF16)
        v_ref[bb] = v[rows]


def _memory_kv(mem, mem_norm_g, wkv_b):
    L = wkv_b.shape[0]
    B = mem.shape[0]
    assert B % KV_BATCHES == 0
    return pl.pallas_call(
        _kv_kernel,
        grid=(L, B // KV_BATCHES),
        in_specs=[
            pl.BlockSpec((KV_BATCHES * N_MEM, D_MODEL), lambda l, b: (b, 0)),
            pl.BlockSpec((1, D_MODEL), lambda l, b: (0, 0)),
            pl.BlockSpec((None, D_MODEL, 2 * D_MODEL), lambda l, b: (l, 0, 0)),
        ],
        out_specs=[
            pl.BlockSpec((None, KV_BATCHES, D_MODEL, N_MEM), lambda l, b: (l, b, 0, 0)),
            pl.BlockSpec((None, KV_BATCHES, N_MEM, D_MODEL), lambda l, b: (l, b, 0, 0)),
        ],
        out_shape=[
            jax.ShapeDtypeStruct((L, B, D_MODEL, N_MEM), BF16),
            jax.ShapeDtypeStruct((L, B, N_MEM, D_MODEL), BF16),
        ],
        compiler_params=pltpu.CompilerParams(vmem_limit_bytes=VMEM_LIMIT_BYTES),
        name="memory_kv",
    )(mem.reshape(B * N_MEM, D_MODEL), mem_norm_g.reshape(1, D_MODEL), wkv_b)


def _rope(t, cosf, sinf):
    return t * cosf + pltpu.roll(t, RET_HEAD_DIM // 2, 1) * sinf


def _trailing_window_sum(ext, wlen):
    span = 1
    while span < wlen:
        ext = ext + pltpu.roll(ext, span, 0)
        span *= 2
    return ext


def _mix_kernel(x_ref, cos_ref, sin_ref, ng_ref, win_ref, gng_ref, pw_ref, ps_ref, wout_ref,
                o_ref, state_ref, ubuf_ref, ybuf_ref, *, T):
    s = pl.program_id(1)

    @pl.when(s == 0)
    def _():
        state_ref[...] = jnp.zeros_like(state_ref)
        ubuf_ref[0:POOL_TAIL, :] = jnp.zeros((POOL_TAIL, POOL_WIDTH), F32)

    R = RET_WIDTH
    row = lax.broadcasted_iota(jnp.int32, (CHUNK, CHUNK), 0)
    col = lax.broadcasted_iota(jnp.int32, (CHUNK, CHUNK), 1)
    diff = (row - col).astype(F32)
    rowf = row.astype(F32)
    k_scale = RET_HEAD_DIM ** -0.5
    decay_consts = []
    for hd in range(RET_HEADS):
        lg = math.log1p(-(2.0 ** (-5 - hd)))
        dmat = jnp.where(diff >= 0, jnp.exp(lg * jnp.maximum(diff, 0.0)), 0.0) * k_scale
        zeta = jnp.exp(lg * (CHUNK - 1 - rowf)) * k_scale
        xi = jnp.exp(lg * (rowf + 1.0))
        decay_consts.append((dmat, zeta, xi, math.exp(lg * CHUNK)))

    n_blocks = T // ROW_BLOCK
    n_chunks = ROW_BLOCK // CHUNK
    head_lanes = [slice(hd * RET_HEAD_DIM, (hd + 1) * RET_HEAD_DIM) for hd in range(RET_HEADS)]
    st = [dict() for _ in range(n_blocks)]

    def rows_of(i):
        return slice(i * ROW_BLOCK, (i + 1) * ROW_BLOCK)

    def chunk_rows(c):
        return slice(c * CHUNK, (c + 1) * CHUNK)

    def project_qk(i):
        b = st[i]
        b["h"] = _rmsnorm(x_ref[0, rows_of(i)], ng_ref[...]).astype(BF16)
        b["q"] = _dot(b["h"], win_ref[:, 0:R])
        b["k"] = _dot(b["h"], win_ref[:, R:2 * R])

    def project_rest(i):
        b = st[i]
        b["v"] = _dot(b["h"], win_ref[:, 2 * R:3 * R]).astype(BF16)
        b["g"] = _dot(b["h"], win_ref[:, 3 * R:4 * R])
        r0 = i * ROW_BLOCK
        ubuf_ref[POOL_TAIL + r0:POOL_TAIL + r0 + ROW_BLOCK, :] = _dot(
            b["h"], win_ref[:, 4 * R:4 * R + POOL_WIDTH])

    def retention_scores(i):
        b = st[i]
        cosf = cos_ref[0, rows_of(i)]
        sinf = sin_ref[0, rows_of(i)]
        b["qc"], b["scores"], b["kv"] = {}, {}, {}
        for hd, lanes in enumerate(head_lanes):
            zeta = decay_consts[hd][1]
            qr = _rope(b["q"][:, lanes], cosf, sinf).astype(BF16)
            kr = _rope(b["k"][:, lanes], cosf, sinf)
            for c in range(n_chunks):
                rows = chunk_rows(c)
                vc = b["v"][rows, lanes]
                b["qc"][hd, c] = qr[rows]
                b["scores"][hd, c] = _dot_nt(qr[rows], kr[rows].astype(BF16))
                b["kv"][hd, c] = _dot_tn((kr[rows] * zeta).astype(BF16), vc)

    def retention_states(i):
        b = st[i]
        b["p"], b["state_in"] = {}, {}
        for hd in range(RET_HEADS):
            dmat, _, _, chunk_decay = decay_consts[hd]
            state = state_ref[hd]
            for c in range(n_chunks):
                b["p"][hd, c] = (b["scores"][hd, c] * dmat).astype(BF16)
                b["state_in"][hd, c] = state.astype(BF16)
                state = state * chunk_decay + b["kv"][hd, c]
            state_ref[hd] = state

    def retention_outputs(i):
        b = st[i]
        r0 = i * ROW_BLOCK
        for hd, lanes in enumerate(head_lanes):
            xi = decay_consts[hd][2]
            gn = gng_ref[:, lanes]
            for c in range(n_chunks):
                rows = chunk_rows(c)
                intra = _dot(b["p"][hd, c], b["v"][rows, lanes])
                cross = _dot(b["qc"][hd, c], b["state_in"][hd, c]) * xi
                ret = intra + cross
                ms = jnp.mean(ret * ret, axis=-1, keepdims=True)
                yn = ret * lax.rsqrt(ms + EPS) * gn
                gc = b["g"][rows, lanes]
                ybuf_ref[r0 + c * CHUNK:r0 + (c + 1) * CHUNK, lanes] = (
                    gc * jax.nn.sigmoid(gc) * yn).astype(BF16)

    def pooling(i):
        r0 = i * ROW_BLOCK
        t_next = (s * T + r0 + 1 + lax.broadcasted_iota(jnp.int32, (ROW_BLOCK, POOL_GROUP), 0)
                  ).astype(F32)
        for gi, wlen in enumerate(POOL_WINDOWS):
            lanes = slice(gi * POOL_GROUP, (gi + 1) * POOL_GROUP)
            ext = ubuf_ref[r0:r0 + ROW_BLOCK + POOL_TAIL, lanes]
            wsum = _trailing_window_sum(ext, wlen)[POOL_TAIL:]
            inv_count = 1.0 / jnp.minimum(t_next, float(wlen))
            pooled = (wsum * inv_count - ext[POOL_TAIL:]).astype(BF16)
            og = _dot(pooled, pw_ref[gi]) * ps_ref[:, lanes]
            ybuf_ref[rows_of(i), R + gi * POOL_GROUP:R + (gi + 1) * POOL_GROUP] = og.astype(BF16)

    def output_pool_half(i):
        o_ref[0, rows_of(i)] = x_ref[0, rows_of(i)] + _dot(ybuf_ref[rows_of(i), R:], wout_ref[R:, :])

    def output_retention_half(i):
        o_ref[0, rows_of(i)] += _dot(ybuf_ref[rows_of(i), :R], wout_ref[:R, :])

    project_qk(0)
    project_rest(0)
    retention_scores(0)
    for i in range(n_blocks):
        has_next = i + 1 < n_blocks
        if has_next:
            project_qk(i + 1)
        retention_states(i)
        retention_outputs(i)
        pooling(i)
        if has_next:
            project_rest(i + 1)
            retention_scores(i + 1)
        if i > 0:
            output_retention_half(i - 1)
        output_pool_half(i)
    output_retention_half(n_blocks - 1)

    ubuf_ref[0:POOL_TAIL, :] = ubuf_ref[T:T + POOL_TAIL, :]


def _mixer(x, cosf, sinf, l, mix_norm_g, w_in_b, ret_gn_g, pool_w_b, pool_scale, w_out_b):
    B, S, D = x.shape
    T = MIX_TILE
    tile = pl.BlockSpec((1, T, D), lambda b, s: (b, s, 0))
    rope = pl.BlockSpec((1, T, RET_HEAD_DIM), lambda b, s: (b, s, 0))
    return pl.pallas_call(
        functools.partial(_mix_kernel, T=T),
        grid=(B, S // T),
        in_specs=[
            tile, rope, rope,
            _row_spec(D, l),
            _const_spec((None, D, w_in_b.shape[-1]), (l, 0, 0)),
            _row_spec(RET_WIDTH, l),
            _const_spec((None, len(POOL_WINDOWS), POOL_GROUP, POOL_GROUP), (l, 0, 0, 0)),
            _row_spec(POOL_WIDTH, l),
            _const_spec((None, D, D), (l, 0, 0)),
        ],
        out_specs=tile,
        out_shape=jax.ShapeDtypeStruct(x.shape, F32),
        scratch_shapes=[
            pltpu.VMEM((RET_HEADS, RET_HEAD_DIM, RET_HEAD_DIM), F32),
            pltpu.VMEM((T + POOL_TAIL, POOL_WIDTH), F32),
            pltpu.VMEM((T, D), BF16),
        ],
        compiler_params=pltpu.CompilerParams(
            dimension_semantics=("parallel", "arbitrary"),
            vmem_limit_bytes=VMEM_LIMIT_BYTES),
        name=f"mixer_l{l}",
    )(x, cosf, sinf, mix_norm_g, w_in_b, ret_gn_g, pool_w_b, pool_scale, w_out_b)


def _xa_kernel(x_ref, ng_ref, wq_ref, kt_ref, v_ref, wo_ref, o_ref, obuf_ref, *, T):
    n_blocks = T // ROW_BLOCK
    heads = [slice(hd * XA_HEAD_DIM, (hd + 1) * XA_HEAD_DIM) for hd in range(XA_HEADS)]
    scores = [None] * n_blocks

    def rows_of(i):
        return slice(i * ROW_BLOCK, (i + 1) * ROW_BLOCK)

    def score(i):
        h = _rmsnorm(x_ref[0, rows_of(i)], ng_ref[...]).astype(BF16)
        q = _dot(h, wq_ref[...])
        scores[i] = [_dot(q[:, dims].astype(BF16), kt_ref[0, dims, :]) * (XA_HEAD_DIM ** -0.5)
                     for dims in heads]

    def attend(i):
        for dims, sc in zip(heads, scores[i]):
            p = jnp.exp(sc - jnp.max(sc, axis=-1, keepdims=True))
            p = p * (1.0 / jnp.sum(p, axis=-1, keepdims=True))
            obuf_ref[rows_of(i), dims] = _dot(p.astype(BF16), v_ref[0, :, dims]).astype(BF16)

    def project_out(i):
        o_ref[0, rows_of(i)] = x_ref[0, rows_of(i)] + _dot(obuf_ref[rows_of(i), :], wo_ref[...])

    score(0)
    for i in range(n_blocks):
        if i + 1 < n_blocks:
            score(i + 1)
        attend(i)
        if i > 0:
            project_out(i - 1)
    project_out(n_blocks - 1)


def _cross_attention(x, l, xa_norm_g, wq_b, kt, v, wo_b):
    B, S, D = x.shape
    T = XA_TILE
    tile = pl.BlockSpec((1, T, D), lambda b, s: (b, s, 0))
    return pl.pallas_call(
        functools.partial(_xa_kernel, T=T),
        grid=(B, S // T),
        in_specs=[
            tile,
            _row_spec(D, l),
            _const_spec((None, D, D), (l, 0, 0)),
            pl.BlockSpec((None, 1, D, N_MEM), lambda b, s: (l, b, 0, 0)),
            pl.BlockSpec((None, 1, N_MEM, D), lambda b, s: (l, b, 0, 0)),
            _const_spec((None, D, D), (l, 0, 0)),
        ],
        out_specs=tile,
        out_shape=jax.ShapeDtypeStruct(x.shape, F32),
        scratch_shapes=[pltpu.VMEM((T, D), BF16)],
        compiler_params=pltpu.CompilerParams(
            dimension_semantics=("parallel", "arbitrary"),
            vmem_limit_bytes=VMEM_LIMIT_BYTES),
        name=f"cross_attn_l{l}",
    )(x, xa_norm_g, wq_b, kt, v, wo_b)


def _shift_rows(a, prev, k):
    rolled = pltpu.roll(a, k, 0)
    row = lax.broadcasted_iota(jnp.int32, prev.shape, 0)
    head = jnp.where(row < k, pltpu.roll(prev, k, 0), rolled[0:CONV_TAIL])
    return jnp.concatenate([head, rolled[CONV_TAIL:]], axis=0)


def _causal_conv(a, tail_ref, cw):
    prev = tail_ref[...]
    tail_ref[...] = a[a.shape[0] - CONV_TAIL:]
    return (cw[0:1] * _shift_rows(a, prev, 2) + cw[1:2] * _shift_rows(a, prev, 1)
            + cw[2:3] * a + cw[3:4])


def _ffn_kernel(x_ref, ng_ref, wup_ref, cw_ref, wd_ref, fg_ref, o_ref,
                hbuf_ref, act_ref, tail_ref, *, T, final):
    @pl.when(pl.program_id(1) == 0)
    def _():
        tail_ref[...] = jnp.zeros_like(tail_ref)

    n_blocks = T // ROW_BLOCK
    n_down = D_MODEL // DOWN_COLS

    def rows_of(i):
        return slice(i * ROW_BLOCK, (i + 1) * ROW_BLOCK)

    def normalise(i):
        hbuf_ref[rows_of(i), :] = _rmsnorm(x_ref[0, rows_of(i)], ng_ref[...]).astype(BF16)

    def up_block(i, j):
        h = hbuf_ref[rows_of(i), :]
        gate = slice(j * FF_BLOCK, (j + 1) * FF_BLOCK)
        val = slice(D_FF + j * FF_BLOCK, D_FF + (j + 1) * FF_BLOCK)
        w = jnp.concatenate([wup_ref[:, gate], wup_ref[:, val]], axis=1)
        taps = jnp.concatenate([cw_ref[:, gate], cw_ref[:, val]], axis=1)
        c = _causal_conv(_dot(h, w), tail_ref.at[j], taps)
        cg, cv = c[:, :FF_BLOCK], c[:, FF_BLOCK:]
        act_ref[rows_of(i), gate] = (cg * jax.nn.sigmoid(cg) * cv).astype(BF16)

    def down_cols(i, n):
        cols = slice(n * DOWN_COLS, (n + 1) * DOWN_COLS)
        o_ref[0, rows_of(i), cols] = x_ref[0, rows_of(i), cols] + _dot(
            act_ref[rows_of(i), :], wd_ref[:, cols])

    def finish(i):
        if final:
            o_ref[0, rows_of(i)] = _rmsnorm(o_ref[0, rows_of(i)], fg_ref[...])

    down_after = [round((n + 1) * N_FF_BLOCKS / n_down) - 1 for n in range(n_down)]
    normalise(0)
    for i in range(n_blocks):
        if i + 1 < n_blocks:
            normalise(i + 1)
        for j in range(N_FF_BLOCKS):
            up_block(i, j)
            if i > 0:
                for n in range(n_down):
                    if down_after[n] == j:
                        down_cols(i - 1, n)
        if i > 0:
            finish(i - 1)
    for n in range(n_down):
        down_cols(n_blocks - 1, n)
    finish(n_blocks - 1)


def _ffn(x, l, ffn_norm_g, wup_b, conv_taps, wd_b, final_g, final):
    B, S, D = x.shape
    T = FFN_TILE
    NB = N_FF_BLOCKS
    tile = pl.BlockSpec((1, T, D), lambda b, s: (b, s, 0))
    return pl.pallas_call(
        functools.partial(_ffn_kernel, T=T, final=final),
        grid=(B, S // T),
        in_specs=[
            tile,
            _row_spec(D, l),
            _const_spec((None, D, 2 * D_FF), (l, 0, 0)),
            _const_spec((None, CONV_TAIL, 2 * D_FF), (l, 0, 0)),
            _const_spec((None, D_FF, D), (l, 0, 0)),
            _const_spec((1, D), (0, 0)),
        ],
        out_specs=tile,
        out_shape=jax.ShapeDtypeStruct(x.shape, F32),
        scratch_shapes=[
            pltpu.VMEM((T, D), BF16),
            pltpu.VMEM((T, D_FF), BF16),
            pltpu.VMEM((NB, CONV_TAIL, 2 * FF_BLOCK), F32),
        ],
        compiler_params=pltpu.CompilerParams(
            dimension_semantics=("parallel", "arbitrary"),
            vmem_limit_bytes=VMEM_LIMIT_BYTES),
        name=f"ffn_l{l}",
    )(x, ffn_norm_g, wup_b, conv_taps, wd_b, final_g)


def _conv_taps(ffn_conv_w, ffn_conv_b):
    taps = jnp.concatenate([ffn_conv_w, ffn_conv_b[:, None, :]], axis=1)
    return jnp.pad(taps, ((0, 0), (0, CONV_TAIL - taps.shape[1]), (0, 0)))


def kernel(x, mem, positions, mix_norm_g, w_in, ret_gn_g, pool_w, pool_scale, w_out, xa_norm_g,
           mem_norm_g, xa_wq, xa_wkv, xa_wo, ffn_norm_g, ffn_w_up, ffn_conv_w, ffn_conv_b,
           ffn_w_down, final_norm_g):
    L = w_in.shape[0]
    S = x.shape[1]
    assert S % MIX_TILE == 0 and S % XA_TILE == 0 and S % FFN_TILE == 0
    assert MIX_TILE % ROW_BLOCK == 0 and XA_TILE % ROW_BLOCK == 0 and ROW_BLOCK % CHUNK == 0
    w_in_b, w_out_b, pool_w_b = w_in.astype(BF16), w_out.astype(BF16), pool_w.astype(BF16)
    wq_b, wkv_b, wo_b = xa_wq.astype(BF16), xa_wkv.astype(BF16), xa_wo.astype(BF16)
    wup_b, wd_b = ffn_w_up.astype(BF16), ffn_w_down.astype(BF16)
    conv_taps = _conv_taps(ffn_conv_w, ffn_conv_b)
    final_g = final_norm_g.reshape(1, D_MODEL)
    mix_norm_g, ret_gn_g, pool_scale = _rows(mix_norm_g), _rows(ret_gn_g), _rows(pool_scale)
    xa_norm_g, ffn_norm_g = _rows(xa_norm_g), _rows(ffn_norm_g)

    cosf, sinf = _rope_tables(positions)
    kt, v = _memory_kv(mem, mem_norm_g, wkv_b)
    for l in range(L):
        x = _mixer(x, cosf, sinf, l, mix_norm_g, w_in_b, ret_gn_g, pool_w_b, pool_scale, w_out_b)
        x = _cross_attention(x, l, xa_norm_g, wq_b, kt, v, wo_b)
        x = _ffn(x, l, ffn_norm_g, wup_b, conv_taps, wd_b, final_g, final=(l == L - 1))
    return x
```

```python
import functools
import math

import jax
import jax.numpy as jnp
from jax import lax
from jax.experimental import pallas as pl
from jax.experimental.pallas import tpu as pltpu

F32 = jnp.float32
BF16 = jnp.bfloat16

D_MODEL = 1024
N_MEM = 256
RET_WIDTH = 512
RET_HEADS = 4
RET_HEAD_DIM = 128
POOL_WIDTH = 512
POOL_WINDOWS = (2, 4, 8, 16)
POOL_GROUP = 128
POOL_TAIL = 16
CHUNK = 128
ROPE_BASE = 10000.0
XA_HEADS = 4
XA_HEAD_DIM = 256
D_FF = 2816
FF_BLOCK = 128
N_FF_BLOCKS = D_FF // FF_BLOCK
SUBLANES = 8
LANES = 128
N_SLABS = D_MODEL // LANES
CONV_WIDTH = 3
CONV_CARRY = (CONV_WIDTH - 1) * SUBLANES
TAP_ROWS = SUBLANES
DOWN_COLS = 256
KV_BATCHES = 4
EPS = 1e-6

ROW_BLOCK = 512
MIX_TILE = 1024
XA_TILE = 1024
FFN_TILE = 512
VMEM_LIMIT_BYTES = 56 * 1024 * 1024


def _rmsnorm(x, g):
    ms = jnp.mean(x * x, axis=-1, keepdims=True)
    return x * lax.rsqrt(ms + EPS) * g


def _dot(a, b):
    return jnp.dot(a, b, preferred_element_type=F32)


def _dot_nt(a, b):
    return lax.dot_general(a, b, (((1,), (1,)), ((), ())), preferred_element_type=F32)


def _dot_tn(a, b):
    return lax.dot_general(a, b, (((0,), (0,)), ((), ())), preferred_element_type=F32)


def _const_spec(shape, index):
    return pl.BlockSpec(shape, lambda b, s: index, pipeline_mode=pl.Buffered(1))


def _row_spec(n, l):
    return _const_spec((None, 1, n), (l, 0, 0))


def _rows(a):
    return a.reshape(a.shape[0], 1, a.shape[1])


def _rope_kernel(pos_ref, cos_ref, sin_ref):
    T = pos_ref.shape[1]
    lane = lax.broadcasted_iota(jnp.int32, (1, RET_HEAD_DIM), 1)
    half = RET_HEAD_DIM // 2
    low = lane < half
    j2 = (2 * jnp.where(low, lane, lane - half)).astype(F32)
    inv_freq = jnp.exp(-math.log(ROPE_BASE) * j2 / RET_HEAD_DIM)
    pos = jnp.where(low, pos_ref[0, 0:T // 2], pos_ref[0, T // 2:T])
    ang = pos * inv_freq
    cos = jnp.cos(ang)
    sin = jnp.sin(ang)
    cos_sw = pltpu.roll(cos, half, 1)
    sin_sw = pltpu.roll(sin, half, 1)
    cos_ref[0, 0:T // 2] = jnp.where(low, cos, cos_sw)
    cos_ref[0, T // 2:T] = jnp.where(low, cos_sw, cos)
    sin_ref[0, 0:T // 2] = jnp.where(low, -sin, sin_sw)
    sin_ref[0, T // 2:T] = jnp.where(low, -sin_sw, sin)


def _rope_tables(positions):
    B, S = positions.shape
    T = ROW_BLOCK
    pos = positions.astype(F32)[..., None]
    out = jax.ShapeDtypeStruct((B, S, RET_HEAD_DIM), F32)
    return pl.pallas_call(
        _rope_kernel,
        grid=(B, S // T),
        in_specs=[pl.BlockSpec((1, T, 1), lambda b, s: (b, s, 0))],
        out_specs=[pl.BlockSpec((1, T, RET_HEAD_DIM), lambda b, s: (b, s, 0))] * 2,
        out_shape=[out, out],
        name="rope_tables",
    )(pos)


def _kv_kernel(mem_ref, g_ref, wkv_ref, kt_ref, v_ref):
    mem_n = _rmsnorm(mem_ref[...], g_ref[...]).astype(BF16)
    k = _dot(mem_n, wkv_ref[:, :D_MODEL])
    v = _dot(mem_n, wkv_ref[:, D_MODEL:]).astype(BF16)
    for bb in range(KV_BATCHES):
        rows = slice(bb * N_MEM, (bb + 1) * N_MEM)
        kt_ref[bb] = k[rows].T.astype(BF16)
        v_ref[bb] = v[rows]


def _memory_kv(mem, mem_norm_g, wkv_b):
    L = wkv_b.shape[0]
    B = mem.shape[0]
    assert B % KV_BATCHES == 0
    return pl.pallas_call(
        _kv_kernel,
        grid=(L, B // KV_BATCHES),
        in_specs=[
            pl.BlockSpec((KV_BATCHES * N_MEM, D_MODEL), lambda l, b: (b, 0)),
            pl.BlockSpec((1, D_MODEL), lambda l, b: (0, 0)),
            pl.BlockSpec((None, D_MODEL, 2 * D_MODEL), lambda l, b: (l, 0, 0)),
        ],
        out_specs=[
            pl.BlockSpec((None, KV_BATCHES, D_MODEL, N_MEM), lambda l, b: (l, b, 0, 0)),
            pl.BlockSpec((None, KV_BATCHES, N_MEM, D_MODEL), lambda l, b: (l, b, 0, 0)),
        ],
        out_shape=[
            jax.ShapeDtypeStruct((L, B, D_MODEL, N_MEM), BF16),
            jax.ShapeDtypeStruct((L, B, N_MEM, D_MODEL), BF16),
        ],
        compiler_params=pltpu.CompilerParams(vmem_limit_bytes=VMEM_LIMIT_BYTES),
        name="memory_kv",
    )(mem.reshape(B * N_MEM, D_MODEL), mem_norm_g.reshape(1, D_MODEL), wkv_b)


def _rope(t, cosf, sinf):
    return t * cosf + pltpu.roll(t, RET_HEAD_DIM // 2, 1) * sinf


def _trailing_window_sum(ext, wlen):
    span = 1
    while span < wlen:
        ext = ext + pltpu.roll(ext, span, 0)
        span *= 2
    return ext


def _mix_kernel(x_ref, cos_ref, sin_ref, ng_ref, win_ref, gng_ref, pw_ref, ps_ref, wout_ref,
                o_ref, state_ref, ubuf_ref, ybuf_ref, *, T):
    s = pl.program_id(1)

    @pl.when(s == 0)
    def _():
        state_ref[...] = jnp.zeros_like(state_ref)
        ubuf_ref[0:POOL_TAIL, :] = jnp.zeros((POOL_TAIL, POOL_WIDTH), F32)

    R = RET_WIDTH
    row = lax.broadcasted_iota(jnp.int32, (CHUNK, CHUNK), 0)
    col = lax.broadcasted_iota(jnp.int32, (CHUNK, CHUNK), 1)
    diff = (row - col).astype(F32)
    rowf = row.astype(F32)
    k_scale = RET_HEAD_DIM ** -0.5
    decay_consts = []
    for hd in range(RET_HEADS):
        lg = math.log1p(-(2.0 ** (-5 - hd)))
        dmat = jnp.where(diff >= 0, jnp.exp(lg * jnp.maximum(diff, 0.0)), 0.0) * k_scale
        zeta = jnp.exp(lg * (CHUNK - 1 - rowf)) * k_scale
        xi = jnp.exp(lg * (rowf + 1.0))
        decay_consts.append((dmat, zeta, xi, math.exp(lg * CHUNK)))

    n_blocks = T // ROW_BLOCK
    n_chunks = ROW_BLOCK // CHUNK
    head_lanes = [slice(hd * RET_HEAD_DIM, (hd + 1) * RET_HEAD_DIM) for hd in range(RET_HEADS)]
    st = [dict() for _ in range(n_blocks)]

    def rows_of(i):
        return slice(i * ROW_BLOCK, (i + 1) * ROW_BLOCK)

    def chunk_rows(c):
        return slice(c * CHUNK, (c + 1) * CHUNK)

    def project_qk(i):
        b = st[i]
        b["h"] = _rmsnorm(x_ref[0, rows_of(i)], ng_ref[...]).astype(BF16)
        b["q"] = _dot(b["h"], win_ref[:, 0:R])
        b["k"] = _dot(b["h"], win_ref[:, R:2 * R])

    def project_rest(i):
        b = st[i]
        b["v"] = _dot(b["h"], win_ref[:, 2 * R:3 * R]).astype(BF16)
        b["g"] = _dot(b["h"], win_ref[:, 3 * R:4 * R])
        r0 = i * ROW_BLOCK
        ubuf_ref[POOL_TAIL + r0:POOL_TAIL + r0 + ROW_BLOCK, :] = _dot(
            b["h"], win_ref[:, 4 * R:4 * R + POOL_WIDTH])

    def retention_scores(i):
        b = st[i]
        cosf = cos_ref[0, rows_of(i)]
        sinf = sin_ref[0, rows_of(i)]
        b["qc"], b["scores"], b["kv"] = {}, {}, {}
        for hd, lanes in enumerate(head_lanes):
            zeta = decay_consts[hd][1]
            qr = _rope(b["q"][:, lanes], cosf, sinf).astype(BF16)
            kr = _rope(b["k"][:, lanes], cosf, sinf)
            for c in range(n_chunks):
                rows = chunk_rows(c)
                vc = b["v"][rows, lanes]
                b["qc"][hd, c] = qr[rows]
                b["scores"][hd, c] = _dot_nt(qr[rows], kr[rows].astype(BF16))
                b["kv"][hd, c] = _dot_tn((kr[rows] * zeta).astype(BF16), vc)

    def retention_states(i):
        b = st[i]
        b["p"], b["state_in"] = {}, {}
        for hd in range(RET_HEADS):
            dmat, _, _, chunk_decay = decay_consts[hd]
            state = state_ref[hd]
            for c in range(n_chunks):
                b["p"][hd, c] = (b["scores"][hd, c] * dmat).astype(BF16)
                b["state_in"][hd, c] = state.astype(BF16)
                state = state * chunk_decay + b["kv"][hd, c]
            state_ref[hd] = state

    def retention_outputs(i):
        b = st[i]
        r0 = i * ROW_BLOCK
        for hd, lanes in enumerate(head_lanes):
            xi = decay_consts[hd][2]
            gn = gng_ref[:, lanes]
            for c in range(n_chunks):
                rows = chunk_rows(c)
                intra = _dot(b["p"][hd, c], b["v"][rows, lanes])
                cross = _dot(b["qc"][hd, c], b["state_in"][hd, c]) * xi
                ret = intra + cross
                ms = jnp.mean(ret * ret, axis=-1, keepdims=True)
                yn = ret * lax.rsqrt(ms + EPS) * gn
                gc = b["g"][rows, lanes]
                ybuf_ref[r0 + c * CHUNK:r0 + (c + 1) * CHUNK, lanes] = (
                    gc * jax.nn.sigmoid(gc) * yn).astype(BF16)

    def pooling(i):
        r0 = i * ROW_BLOCK
        t_next = (s * T + r0 + 1 + lax.broadcasted_iota(jnp.int32, (ROW_BLOCK, POOL_GROUP), 0)
                  ).astype(F32)
        for gi, wlen in enumerate(POOL_WINDOWS):
            lanes = slice(gi * POOL_GROUP, (gi + 1) * POOL_GROUP)
            ext = ubuf_ref[r0:r0 + ROW_BLOCK + POOL_TAIL, lanes]
            wsum = _trailing_window_sum(ext, wlen)[POOL_TAIL:]
            inv_count = 1.0 / jnp.minimum(t_next, float(wlen))
            pooled = (wsum * inv_count - ext[POOL_TAIL:]).astype(BF16)
            og = _dot(pooled, pw_ref[gi]) * ps_ref[:, lanes]
            ybuf_ref[rows_of(i), R + gi * POOL_GROUP:R + (gi + 1) * POOL_GROUP] = og.astype(BF16)

    def output_pool_half(i):
        o_ref[0, rows_of(i)] = x_ref[0, rows_of(i)] + _dot(ybuf_ref[rows_of(i), R:], wout_ref[R:, :])

    def output_retention_half(i):
        o_ref[0, rows_of(i)] += _dot(ybuf_ref[rows_of(i), :R], wout_ref[:R, :])

    project_qk(0)
    project_rest(0)
    retention_scores(0)
    for i in range(n_blocks):
        has_next = i + 1 < n_blocks
        if has_next:
            project_qk(i + 1)
        retention_states(i)
        retention_outputs(i)
        pooling(i)
        if has_next:
            project_rest(i + 1)
            retention_scores(i + 1)
        if i > 0:
            output_retention_half(i - 1)
        output_pool_half(i)
    output_retention_half(n_blocks - 1)

    ubuf_ref[0:POOL_TAIL, :] = ubuf_ref[T:T + POOL_TAIL, :]


def _mixer(x, cosf, sinf, l, mix_norm_g, w_in_b, ret_gn_g, pool_w_b, pool_scale, w_out_b):
    B, S, D = x.shape
    T = MIX_TILE
    tile = pl.BlockSpec((1, T, D), lambda b, s: (b, s, 0))
    rope = pl.BlockSpec((1, T, RET_HEAD_DIM), lambda b, s: (b, s, 0))
    return pl.pallas_call(
        functools.partial(_mix_kernel, T=T),
        grid=(B, S // T),
        in_specs=[
            tile, rope, rope,
            _row_spec(D, l),
            _const_spec((None, D, w_in_b.shape[-1]), (l, 0, 0)),
            _row_spec(RET_WIDTH, l),
            _const_spec((None, len(POOL_WINDOWS), POOL_GROUP, POOL_GROUP), (l, 0, 0, 0)),
            _row_spec(POOL_WIDTH, l),
            _const_spec((None, D, D), (l, 0, 0)),
        ],
        out_specs=tile,
        out_shape=jax.ShapeDtypeStruct(x.shape, F32),
        scratch_shapes=[
            pltpu.VMEM((RET_HEADS, RET_HEAD_DIM, RET_HEAD_DIM), F32),
            pltpu.VMEM((T + POOL_TAIL, POOL_WIDTH), F32),
            pltpu.VMEM((T, D), BF16),
        ],
        compiler_params=pltpu.CompilerParams(
            dimension_semantics=("parallel", "arbitrary"),
            vmem_limit_bytes=VMEM_LIMIT_BYTES),
        name=f"mixer_l{l}",
    )(x, cosf, sinf, mix_norm_g, w_in_b, ret_gn_g, pool_w_b, pool_scale, w_out_b)


def _xa_kernel(x_ref, ng_ref, wq_ref, kt_ref, v_ref, wo_ref, o_ref, obuf_ref, *, T):
    n_blocks = T // ROW_BLOCK
    heads = [slice(hd * XA_HEAD_DIM, (hd + 1) * XA_HEAD_DIM) for hd in range(XA_HEADS)]
    scores = [None] * n_blocks

    def rows_of(i):
        return slice(i * ROW_BLOCK, (i + 1) * ROW_BLOCK)

    def score(i):
        h = _rmsnorm(x_ref[0, rows_of(i)], ng_ref[...]).astype(BF16)
        q = _dot(h, wq_ref[...])
        scores[i] = [_dot(q[:, dims].astype(BF16), kt_ref[0, dims, :]) * (XA_HEAD_DIM ** -0.5)
                     for dims in heads]

    def attend(i):
        for dims, sc in zip(heads, scores[i]):
            p = jnp.exp(sc - jnp.max(sc, axis=-1, keepdims=True))
            p = p * (1.0 / jnp.sum(p, axis=-1, keepdims=True))
            obuf_ref[rows_of(i), dims] = _dot(p.astype(BF16), v_ref[0, :, dims]).astype(BF16)

    def project_out(i):
        o_ref[0, rows_of(i)] = x_ref[0, rows_of(i)] + _dot(obuf_ref[rows_of(i), :], wo_ref[...])

    score(0)
    for i in range(n_blocks):
        if i + 1 < n_blocks:
            score(i + 1)
        attend(i)
        if i > 0:
            project_out(i - 1)
    project_out(n_blocks - 1)


def _cross_attention(x, l, xa_norm_g, wq_b, kt, v, wo_b):
    B, S, D = x.shape
    T = XA_TILE
    tile = pl.BlockSpec((1, T, D), lambda b, s: (b, s, 0))
    return pl.pallas_call(
        functools.partial(_xa_kernel, T=T),
        grid=(B, S // T),
        in_specs=[
            tile,
            _row_spec(D, l),
            _const_spec((None, D, D), (l, 0, 0)),
            pl.BlockSpec((None, 1, D, N_MEM), lambda b, s: (l, b, 0, 0)),
            pl.BlockSpec((None, 1, N_MEM, D), lambda b, s: (l, b, 0, 0)),
            _const_spec((None, D, D), (l, 0, 0)),
        ],
        out_specs=tile,
        out_shape=jax.ShapeDtypeStruct(x.shape, F32),
        scratch_shapes=[pltpu.VMEM((T, D), BF16)],
        compiler_params=pltpu.CompilerParams(
            dimension_semantics=("parallel", "arbitrary"),
            vmem_limit_bytes=VMEM_LIMIT_BYTES),
        name=f"cross_attn_l{l}",
    )(x, xa_norm_g, wq_b, kt, v, wo_b)


def _delay_steps(a, carry, k):
    group = SUBLANES * SUBLANES
    row = lax.broadcasted_iota(jnp.int32, (SUBLANES, a.shape[1]), 0)
    pieces = []
    prev_tail = carry
    for g0 in range(0, a.shape[0], group):
        grp = a[g0:g0 + group]
        tail = grp[group - k * SUBLANES:]
        for u in range(k):
            rows = slice(u * SUBLANES, (u + 1) * SUBLANES)
            pieces.append(jnp.where(row == 0, pltpu.roll(prev_tail[rows], 1, 0),
                                    pltpu.roll(tail[rows], 1, 0)))
        pieces.append(grp[:group - k * SUBLANES])
        prev_tail = tail
    return jnp.concatenate(pieces, axis=0)


def _causal_conv(a, tail_ref, taps):
    prev = tail_ref[...]
    tail_ref[...] = a[a.shape[0] - CONV_CARRY:]
    a1 = _delay_steps(a, prev[CONV_CARRY - SUBLANES:], 1)
    a2 = _delay_steps(a, prev, 2)
    return taps[0:1] * a2 + taps[1:2] * a1 + taps[2:3] * a + taps[3:4]


def _ffn_kernel(x_ref, ng_ref, wup_ref, cw_ref, wd_ref, fg_ref, o_ref,
                xs_ref, xp_ref, hbuf_ref, act_ref, tail_ref, *, T, final):
    @pl.when(pl.program_id(1) == 0)
    def _():
        tail_ref[...] = jnp.zeros_like(tail_ref)

    n_blocks = T // ROW_BLOCK
    n_down = D_MODEL // DOWN_COLS

    def rows_of(i):
        return slice(i * ROW_BLOCK, (i + 1) * ROW_BLOCK)

    def regroup(src_ref, c, i):
        r0 = i * ROW_BLOCK
        pieces = {}
        for b in range(SUBLANES):
            col = src_ref[c, pl.ds(r0 + b, ROW_BLOCK // SUBLANES, stride=SUBLANES), :]
            for g in range(ROW_BLOCK // (SUBLANES * SUBLANES)):
                pieces[g, b] = col[g * SUBLANES:(g + 1) * SUBLANES]
        return jnp.concatenate([pieces[g, b] for g in range(ROW_BLOCK // (SUBLANES * SUBLANES))
                                for b in range(SUBLANES)], axis=0)

    def gather(i):
        for c in range(N_SLABS):
            xs_ref[c, rows_of(i), :] = x_ref[0, rows_of(i), c * LANES:(c + 1) * LANES]
        for c in range(N_SLABS):
            xp_ref[c, rows_of(i), :] = regroup(xs_ref, c, i)
        hbuf_ref[rows_of(i), :] = _rmsnorm(residual(i), ng_ref[...]).astype(BF16)

    def residual(i):
        return jnp.concatenate([xp_ref[c, rows_of(i), :] for c in range(N_SLABS)], axis=1)

    def up_block(i, j):
        h = hbuf_ref[rows_of(i), :]
        gate = slice(j * FF_BLOCK, (j + 1) * FF_BLOCK)
        val = slice(D_FF + j * FF_BLOCK, D_FF + (j + 1) * FF_BLOCK)
        w = jnp.concatenate([wup_ref[:, gate], wup_ref[:, val]], axis=1)
        taps = jnp.concatenate([cw_ref[:, gate], cw_ref[:, val]], axis=1)
        c = _causal_conv(_dot(h, w), tail_ref.at[j], taps)
        cg, cv = c[:, :FF_BLOCK], c[:, FF_BLOCK:]
        act_ref[rows_of(i), gate] = (cg * jax.nn.sigmoid(cg) * cv).astype(BF16)

    def down_cols(i, n):
        cols = slice(n * DOWN_COLS, (n + 1) * DOWN_COLS)
        down = _dot(act_ref[rows_of(i), :], wd_ref[:, cols])
        for k in range(DOWN_COLS // LANES):
            xp_ref[n * (DOWN_COLS // LANES) + k, rows_of(i), :] += down[:, k * LANES:(k + 1) * LANES]

    def scatter(i):
        if final:
            y = _rmsnorm(residual(i), fg_ref[...])
            for c in range(N_SLABS):
                xp_ref[c, rows_of(i), :] = y[:, c * LANES:(c + 1) * LANES]
        for c in range(N_SLABS):
            o_ref[0, rows_of(i), c * LANES:(c + 1) * LANES] = regroup(xp_ref, c, i)

    down_after = [round((n + 1) * N_FF_BLOCKS / n_down) - 1 for n in range(n_down)]
    gather(0)
    for i in range(n_blocks):
        if i + 1 < n_blocks:
            gather(i + 1)
        for j in range(N_FF_BLOCKS):
            up_block(i, j)
            if i > 0:
                for n in range(n_down):
                    if down_after[n] == j:
                        down_cols(i - 1, n)
        if i > 0:
            scatter(i - 1)
    for n in range(n_down):
        down_cols(n_blocks - 1, n)
    scatter(n_blocks - 1)


def _ffn(x, l, ffn_norm_g, wup_b, conv_taps, wd_b, final_g, final):
    B, S, D = x.shape
    T = FFN_TILE
    NB = N_FF_BLOCKS
    tile = pl.BlockSpec((1, T, D), lambda b, s: (b, s, 0))
    return pl.pallas_call(
        functools.partial(_ffn_kernel, T=T, final=final),
        grid=(B, S // T),
        in_specs=[
            tile,
            _row_spec(D, l),
            _const_spec((None, D, 2 * D_FF), (l, 0, 0)),
            _const_spec((None, TAP_ROWS, 2 * D_FF), (l, 0, 0)),
            _const_spec((None, D_FF, D), (l, 0, 0)),
            _const_spec((1, D), (0, 0)),
        ],
        out_specs=tile,
        out_shape=jax.ShapeDtypeStruct(x.shape, F32),
        scratch_shapes=[
            pltpu.VMEM((N_SLABS, T, LANES), F32),
            pltpu.VMEM((N_SLABS, T, LANES), F32),
            pltpu.VMEM((T, D), BF16),
            pltpu.VMEM((T, D_FF), BF16),
            pltpu.VMEM((NB, CONV_CARRY, 2 * FF_BLOCK), F32),
        ],
        compiler_params=pltpu.CompilerParams(
            dimension_semantics=("parallel", "arbitrary"),
            vmem_limit_bytes=VMEM_LIMIT_BYTES),
        name=f"ffn_l{l}",
    )(x, ffn_norm_g, wup_b, conv_taps, wd_b, final_g)


def _conv_taps(ffn_conv_w, ffn_conv_b):
    taps = jnp.concatenate([ffn_conv_w, ffn_conv_b[:, None, :]], axis=1)
    return jnp.pad(taps, ((0, 0), (0, TAP_ROWS - taps.shape[1]), (0, 0)))


def kernel(x, mem, positions, mix_norm_g, w_in, ret_gn_g, pool_w, pool_scale, w_out, xa_norm_g,
           mem_norm_g, xa_wq, xa_wkv, xa_wo, ffn_norm_g, ffn_w_up, ffn_conv_w, ffn_conv_b,
           ffn_w_down, final_norm_g):
    L = w_in.shape[0]
    S = x.shape[1]
    assert S % MIX_TILE == 0 and S % XA_TILE == 0 and S % FFN_TILE == 0
    assert MIX_TILE % ROW_BLOCK == 0 and XA_TILE % ROW_BLOCK == 0 and ROW_BLOCK % CHUNK == 0
    w_in_b, w_out_b, pool_w_b = w_in.astype(BF16), w_out.astype(BF16), pool_w.astype(BF16)
    wq_b, wkv_b, wo_b = xa_wq.astype(BF16), xa_wkv.astype(BF16), xa_wo.astype(BF16)
    wup_b, wd_b = ffn_w_up.astype(BF16), ffn_w_down.astype(BF16)
    conv_taps = _conv_taps(ffn_conv_w, ffn_conv_b)
    final_g = final_norm_g.reshape(1, D_MODEL)
    mix_norm_g, ret_gn_g, pool_scale = _rows(mix_norm_g), _rows(ret_gn_g), _rows(pool_scale)
    xa_norm_g, ffn_norm_g = _rows(xa_norm_g), _rows(ffn_norm_g)

    cosf, sinf = _rope_tables(positions)
    kt, v = _memory_kv(mem, mem_norm_g, wkv_b)
    for l in range(L):
        x = _mixer(x, cosf, sinf, l, mix_norm_g, w_in_b, ret_gn_g, pool_w_b, pool_scale, w_out_b)
        x = _cross_attention(x, l, xa_norm_g, wq_b, kt, v, wo_b)
        x = _ffn(x, l, ffn_norm_g, wup_b, conv_taps, wd_b, final_g, final=(l == L - 1))
    return x
```

```python
import functools
import math

import jax
import jax.numpy as jnp
from jax import lax
from jax.experimental import pallas as pl
from jax.experimental.pallas import tpu as pltpu

F32 = jnp.float32
BF16 = jnp.bfloat16

D_MODEL = 1024
N_MEM = 256
RET_WIDTH = 512
RET_HEADS = 4
RET_HEAD_DIM = 128
POOL_WIDTH = 512
POOL_WINDOWS = (2, 4, 8, 16)
POOL_GROUP = 128
POOL_TAIL = 16
CHUNK = 128
ROPE_BASE = 10000.0
XA_HEADS = 4
XA_HEAD_DIM = 256
D_FF = 2816
FF_BLOCK = 128
N_FF_BLOCKS = D_FF // FF_BLOCK
CONV_TAIL = 8
KV_BATCHES = 4
EPS = 1e-6

ROW_BLOCK = 512
MIX_TILE = 1024
XA_TILE = 1024
FFN_TILE = 512
VMEM_LIMIT_BYTES = 56 * 1024 * 1024


def _rmsnorm(x, g):
    ms = jnp.mean(x * x, axis=-1, keepdims=True)
    return x * lax.rsqrt(ms + EPS) * g


def _dot(a, b):
    return jnp.dot(a, b, preferred_element_type=F32)


def _dot_nt(a, b):
    return lax.dot_general(a, b, (((1,), (1,)), ((), ())), preferred_element_type=F32)


def _dot_tn(a, b):
    return lax.dot_general(a, b, (((0,), (0,)), ((), ())), preferred_element_type=F32)


def _const_spec(shape, index):
    return pl.BlockSpec(shape, lambda b, s: index, pipeline_mode=pl.Buffered(1))


def _row_spec(n, l):
    return _const_spec((None, 1, n), (l, 0, 0))


def _rows(a):
    return a.reshape(a.shape[0], 1, a.shape[1])


def _rope_kernel(pos_ref, cos_ref, sin_ref):
    T = pos_ref.shape[1]
    lane = lax.broadcasted_iota(jnp.int32, (1, RET_HEAD_DIM), 1)
    half = RET_HEAD_DIM // 2
    low = lane < half
    j2 = (2 * jnp.where(low, lane, lane - half)).astype(F32)
    inv_freq = jnp.exp(-math.log(ROPE_BASE) * j2 / RET_HEAD_DIM)
    pos = jnp.where(low, pos_ref[0, 0:T // 2], pos_ref[0, T // 2:T])
    ang = pos * inv_freq
    cos = jnp.cos(ang)
    sin = jnp.sin(ang)
    cos_sw = pltpu.roll(cos, half, 1)
    sin_sw = pltpu.roll(sin, half, 1)
    cos_ref[0, 0:T // 2] = jnp.where(low, cos, cos_sw)
    cos_ref[0, T // 2:T] = jnp.where(low, cos_sw, cos)
    sin_ref[0, 0:T // 2] = jnp.where(low, -sin, sin_sw)
    sin_ref[0, T // 2:T] = jnp.where(low, -sin_sw, sin)


def _rope_tables(positions):
    B, S = positions.shape
    T = ROW_BLOCK
    pos = positions.astype(F32)[..., None]
    out = jax.ShapeDtypeStruct((B, S, RET_HEAD_DIM), F32)
    return pl.pallas_call(
        _rope_kernel,
        grid=(B, S // T),
        in_specs=[pl.BlockSpec((1, T, 1), lambda b, s: (b, s, 0))],
        out_specs=[pl.BlockSpec((1, T, RET_HEAD_DIM), lambda b, s: (b, s, 0))] * 2,
        out_shape=[out, out],
        name="rope_tables",
    )(pos)


def _kv_kernel(mem_ref, g_ref, wkv_ref, kt_ref, v_ref):
    mem_n = _rmsnorm(mem_ref[...], g_ref[...]).astype(BF16)
    k = _dot(mem_n, wkv_ref[:, :D_MODEL])
    v = _dot(mem_n, wkv_ref[:, D_MODEL:]).astype(BF16)
    for bb in range(KV_BATCHES):
        rows = slice(bb * N_MEM, (bb + 1) * N_MEM)
        kt_ref[bb] = k[rows].T.astype(BF16)
        v_ref[bb] = v[rows]


def _memory_kv(mem, mem_norm_g, wkv_b):
    L = wkv_b.shape[0]
    B = mem.shape[0]
    assert B % KV_BATCHES == 0
    return pl.pallas_call(
        _kv_kernel,
        grid=(L, B // KV_BATCHES),
        in_specs=[
            pl.BlockSpec((KV_BATCHES * N_MEM, D_MODEL), lambda l, b: (b, 0)),
            pl.BlockSpec((1, D_MODEL), lambda l, b: (0, 0)),
            pl.BlockSpec((None, D_MODEL, 2 * D_MODEL), lambda l, b: (l, 0, 0)),
        ],
        out_specs=[
            pl.BlockSpec((None, KV_BATCHES, D_MODEL, N_MEM), lambda l, b: (l, b, 0, 0)),
            pl.BlockSpec((None, KV_BATCHES, N_MEM, D_MODEL), lambda l, b: (l, b, 0, 0)),
        ],
        out_shape=[
            jax.ShapeDtypeStruct((L, B, D_MODEL, N_MEM), BF16),
            jax.ShapeDtypeStruct((L, B, N_MEM, D_MODEL), BF16),
        ],
        compiler_params=pltpu.CompilerParams(vmem_limit_bytes=VMEM_LIMIT_BYTES),
        name="memory_kv",
    )(mem.reshape(B * N_MEM, D_MODEL), mem_norm_g.reshape(1, D_MODEL), wkv_b)


def _rope(t, cosf, sinf):
    return t * cosf + pltpu.roll(t, RET_HEAD_DIM // 2, 1) * sinf


def _trailing_window_sum(ext, wlen):
    span = 1
    while span < wlen:
        ext = ext + pltpu.roll(ext, span, 0)
        span *= 2
    return ext


def _mix_kernel(x_ref, cos_ref, sin_ref, ng_ref, win_ref, gng_ref, pw_ref, ps_ref, wout_ref,
                o_ref, state_ref, ubuf_ref, ybuf_ref, *, T):
    s = pl.program_id(1)

    @pl.when(s == 0)
    def _():
        state_ref[...] = jnp.zeros_like(state_ref)
        ubuf_ref[0:POOL_TAIL, :] = jnp.zeros((POOL_TAIL, POOL_WIDTH), F32)

    R = RET_WIDTH
    row = lax.broadcasted_iota(jnp.int32, (CHUNK, CHUNK), 0)
    col = lax.broadcasted_iota(jnp.int32, (CHUNK, CHUNK), 1)
    diff = (row - col).astype(F32)
    rowf = row.astype(F32)
    k_scale = RET_HEAD_DIM ** -0.5
    decay_consts = []
    for hd in range(RET_HEADS):
        lg = math.log1p(-(2.0 ** (-5 - hd)))
        dmat = jnp.where(diff >= 0, jnp.exp(lg * jnp.maximum(diff, 0.0)), 0.0) * k_scale
        zeta = jnp.exp(lg * (CHUNK - 1 - rowf)) * k_scale
        xi = jnp.exp(lg * (rowf + 1.0))
        decay_consts.append((dmat, zeta, xi, math.exp(lg * CHUNK)))

    n_blocks = T // ROW_BLOCK
    n_chunks = ROW_BLOCK // CHUNK
    head_lanes = [slice(hd * RET_HEAD_DIM, (hd + 1) * RET_HEAD_DIM) for hd in range(RET_HEADS)]
    st = [dict() for _ in range(n_blocks)]

    def rows_of(i):
        return slice(i * ROW_BLOCK, (i + 1) * ROW_BLOCK)

    def chunk_rows(c):
        return slice(c * CHUNK, (c + 1) * CHUNK)

    def project_qk(i):
        b = st[i]
        b["h"] = _rmsnorm(x_ref[0, rows_of(i)], ng_ref[...]).astype(BF16)
        b["q"] = _dot(b["h"], win_ref[:, 0:R])
        b["k"] = _dot(b["h"], win_ref[:, R:2 * R])

    def project_rest(i):
        b = st[i]
        b["v"] = _dot(b["h"], win_ref[:, 2 * R:3 * R]).astype(BF16)
        b["g"] = _dot(b["h"], win_ref[:, 3 * R:4 * R])
        r0 = i * ROW_BLOCK
        ubuf_ref[POOL_TAIL + r0:POOL_TAIL + r0 + ROW_BLOCK, :] = _dot(
            b["h"], win_ref[:, 4 * R:4 * R + POOL_WIDTH])

    def retention_scores(i):
        b = st[i]
        cosf = cos_ref[0, rows_of(i)]
        sinf = sin_ref[0, rows_of(i)]
        b["qc"], b["scores"], b["kv"] = {}, {}, {}
        for hd, lanes in enumerate(head_lanes):
            zeta = decay_consts[hd][1]
            qr = _rope(b["q"][:, lanes], cosf, sinf).astype(BF16)
            kr = _rope(b["k"][:, lanes], cosf, sinf)
            for c in range(n_chunks):
                rows = chunk_rows(c)
                vc = b["v"][rows, lanes]
                b["qc"][hd, c] = qr[rows]
                b["scores"][hd, c] = _dot_nt(qr[rows], kr[rows].astype(BF16))
                b["kv"][hd, c] = _dot_tn((kr[rows] * zeta).astype(BF16), vc)

    def retention_states(i):
        b = st[i]
        b["p"], b["state_in"] = {}, {}
        for hd in range(RET_HEADS):
            dmat, _, _, chunk_decay = decay_consts[hd]
            state = state_ref[hd]
            for c in range(n_chunks):
                b["p"][hd, c] = (b["scores"][hd, c] * dmat).astype(BF16)
                b["state_in"][hd, c] = state.astype(BF16)
                state = state * chunk_decay + b["kv"][hd, c]
            state_ref[hd] = state

    def retention_outputs(i):
        b = st[i]
        r0 = i * ROW_BLOCK
        for hd, lanes in enumerate(head_lanes):
            xi = decay_consts[hd][2]
            gn = gng_ref[:, lanes]
            for c in range(n_chunks):
                rows = chunk_rows(c)
                intra = _dot(b["p"][hd, c], b["v"][rows, lanes])
                cross = _dot(b["qc"][hd, c], b["state_in"][hd, c]) * xi
                ret = intra + cross
                ms = jnp.mean(ret * ret, axis=-1, keepdims=True)
                yn = ret * lax.rsqrt(ms + EPS) * gn
                gc = b["g"][rows, lanes]
                ybuf_ref[r0 + c * CHUNK:r0 + (c + 1) * CHUNK, lanes] = (
                    gc * jax.nn.sigmoid(gc) * yn).astype(BF16)

    def pooling(i):
        r0 = i * ROW_BLOCK
        t_next = (s * T + r0 + 1 + lax.broadcasted_iota(jnp.int32, (ROW_BLOCK, POOL_GROUP), 0)
                  ).astype(F32)
        for gi, wlen in enumerate(POOL_WINDOWS):
            lanes = slice(gi * POOL_GROUP, (gi + 1) * POOL_GROUP)
            ext = ubuf_ref[r0:r0 + ROW_BLOCK + POOL_TAIL, lanes]
            wsum = _trailing_window_sum(ext, wlen)[POOL_TAIL:]
            inv_count = 1.0 / jnp.minimum(t_next, float(wlen))
            pooled = (wsum * inv_count - ext[POOL_TAIL:]).astype(BF16)
            og = _dot(pooled, pw_ref[gi]) * ps_ref[:, lanes]
            ybuf_ref[rows_of(i), R + gi * POOL_GROUP:R + (gi + 1) * POOL_GROUP] = og.astype(BF16)

    def output_pool_half(i):
        o_ref[0, rows_of(i)] = x_ref[0, rows_of(i)] + _dot(ybuf_ref[rows_of(i), R:], wout_ref[R:, :])

    def output_retention_half(i):
        o_ref[0, rows_of(i)] += _dot(ybuf_ref[rows_of(i), :R], wout_ref[:R, :])

    project_qk(0)
    project_rest(0)
    retention_scores(0)
    for i in range(n_blocks):
        has_next = i + 1 < n_blocks
        if has_next:
            project_qk(i + 1)
        retention_states(i)
        retention_outputs(i)
        pooling(i)
        if has_next:
            project_rest(i + 1)
            retention_scores(i + 1)
        if i > 0:
            output_retention_half(i - 1)
        output_pool_half(i)
    output_retention_half(n_blocks - 1)

    ubuf_ref[0:POOL_TAIL, :] = ubuf_ref[T:T + POOL_TAIL, :]


def _mixer(x, cosf, sinf, l, mix_norm_g, w_in_b, ret_gn_g, pool_w_b, pool_scale, w_out_b):
    B, S, D = x.shape
    T = MIX_TILE
    tile = pl.BlockSpec((1, T, D), lambda b, s: (b, s, 0))
    rope = pl.BlockSpec((1, T, RET_HEAD_DIM), lambda b, s: (b, s, 0))
    return pl.pallas_call(
        functools.partial(_mix_kernel, T=T),
        grid=(B, S // T),
        in_specs=[
            tile, rope, rope,
            _row_spec(D, l),
            _const_spec((None, D, w_in_b.shape[-1]), (l, 0, 0)),
            _row_spec(RET_WIDTH, l),
            _const_spec((None, len(POOL_WINDOWS), POOL_GROUP, POOL_GROUP), (l, 0, 0, 0)),
            _row_spec(POOL_WIDTH, l),
            _const_spec((None, D, D), (l, 0, 0)),
        ],
        out_specs=tile,
        out_shape=jax.ShapeDtypeStruct(x.shape, F32),
        scratch_shapes=[
            pltpu.VMEM((RET_HEADS, RET_HEAD_DIM, RET_HEAD_DIM), F32),
            pltpu.VMEM((T + POOL_TAIL, POOL_WIDTH), F32),
            pltpu.VMEM((T, D), BF16),
        ],
        compiler_params=pltpu.CompilerParams(
            dimension_semantics=("parallel", "arbitrary"),
            vmem_limit_bytes=VMEM_LIMIT_BYTES),
        name=f"mixer_l{l}",
    )(x, cosf, sinf, mix_norm_g, w_in_b, ret_gn_g, pool_w_b, pool_scale, w_out_b)


def _xa_kernel(x_ref, ng_ref, wq_ref, kt_ref, v_ref, wo_ref, o_ref, obuf_ref, *, T):
    n_blocks = T // ROW_BLOCK
    heads = [slice(hd * XA_HEAD_DIM, (hd + 1) * XA_HEAD_DIM) for hd in range(XA_HEADS)]
    scores = [None] * n_blocks

    def rows_of(i):
        return slice(i * ROW_BLOCK, (i + 1) * ROW_BLOCK)

    def score(i):
        h = _rmsnorm(x_ref[0, rows_of(i)], ng_ref[...]).astype(BF16)
        q = _dot(h, wq_ref[...])
        scores[i] = [_dot(q[:, dims].astype(BF16), kt_ref[0, dims, :]) * (XA_HEAD_DIM ** -0.5)
                     for dims in heads]

    def attend(i):
        for dims, sc in zip(heads, scores[i]):
            p = jnp.exp(sc - jnp.max(sc, axis=-1, keepdims=True))
            p = p * (1.0 / jnp.sum(p, axis=-1, keepdims=True))
            obuf_ref[rows_of(i), dims] = _dot(p.astype(BF16), v_ref[0, :, dims]).astype(BF16)

    def project_out(i):
        o_ref[0, rows_of(i)] = x_ref[0, rows_of(i)] + _dot(obuf_ref[rows_of(i), :], wo_ref[...])

    score(0)
    for i in range(n_blocks):
        if i + 1 < n_blocks:
            score(i + 1)
        attend(i)
        if i > 0:
            project_out(i - 1)
    project_out(n_blocks - 1)


def _cross_attention(x, l, xa_norm_g, wq_b, kt, v, wo_b):
    B, S, D = x.shape
    T = XA_TILE
    tile = pl.BlockSpec((1, T, D), lambda b, s: (b, s, 0))
    return pl.pallas_call(
        functools.partial(_xa_kernel, T=T),
        grid=(B, S // T),
        in_specs=[
            tile,
            _row_spec(D, l),
            _const_spec((None, D, D), (l, 0, 0)),
            pl.BlockSpec((None, 1, D, N_MEM), lambda b, s: (l, b, 0, 0)),
            pl.BlockSpec((None, 1, N_MEM, D), lambda b, s: (l, b, 0, 0)),
            _const_spec((None, D, D), (l, 0, 0)),
        ],
        out_specs=tile,
        out_shape=jax.ShapeDtypeStruct(x.shape, F32),
        scratch_shapes=[pltpu.VMEM((T, D), BF16)],
        compiler_params=pltpu.CompilerParams(
            dimension_semantics=("parallel", "arbitrary"),
            vmem_limit_bytes=VMEM_LIMIT_BYTES),
        name=f"cross_attn_l{l}",
    )(x, xa_norm_g, wq_b, kt, v, wo_b)


def _shift_rows(a, prev, k):
    rolled = pltpu.roll(a, k, 0)
    row = lax.broadcasted_iota(jnp.int32, prev.shape, 0)
    head = jnp.where(row < k, pltpu.roll(prev, k, 0), rolled[0:CONV_TAIL])
    return jnp.concatenate([head, rolled[CONV_TAIL:]], axis=0)


def _causal_conv(a, tail_ref, cw):
    prev = tail_ref[...]
    tail_ref[...] = a[a.shape[0] - CONV_TAIL:]
    return (cw[0:1] * _shift_rows(a, prev, 2) + cw[1:2] * _shift_rows(a, prev, 1)
            + cw[2:3] * a + cw[3:4])


def _ffn_kernel(x_ref, xn_ref, ng_ref, wup_ref, cw_ref, wd_ref, fg_ref, o_ref,
                hbuf_ref, act_ref, tail_ref, *, final):
    s = pl.program_id(1)
    slot = s % 2

    def normalised(ref):
        return _rmsnorm(ref[0], ng_ref[...]).astype(BF16)

    @pl.when(s == 0)
    def _():
        tail_ref[...] = jnp.zeros_like(tail_ref)
        hbuf_ref[0] = normalised(x_ref)

    for j in range(N_FF_BLOCKS):
        h = hbuf_ref[slot]
        gate = slice(j * FF_BLOCK, (j + 1) * FF_BLOCK)
        val = slice(D_FF + j * FF_BLOCK, D_FF + (j + 1) * FF_BLOCK)
        w = jnp.concatenate([wup_ref[:, gate], wup_ref[:, val]], axis=1)
        taps = jnp.concatenate([cw_ref[:, gate], cw_ref[:, val]], axis=1)
        c = _causal_conv(_dot(h, w), tail_ref.at[j], taps)
        cg, cv = c[:, :FF_BLOCK], c[:, FF_BLOCK:]
        act_ref[:, gate] = (cg * jax.nn.sigmoid(cg) * cv).astype(BF16)

    hbuf_ref[1 - slot] = normalised(xn_ref)
    o_ref[0] = x_ref[0] + _dot(act_ref[...], wd_ref[...])
    if final:
        o_ref[0] = _rmsnorm(o_ref[0], fg_ref[...])


def _ffn(x, l, ffn_norm_g, wup_b, conv_taps, wd_b, final_g, final):
    B, S, D = x.shape
    T = FFN_TILE
    NB = N_FF_BLOCKS
    last_tile = S // T - 1
    tile = pl.BlockSpec((1, T, D), lambda b, s: (b, s, 0))
    next_tile = pl.BlockSpec((1, T, D), lambda b, s: (b, jnp.minimum(s + 1, last_tile), 0))
    return pl.pallas_call(
        functools.partial(_ffn_kernel, final=final),
        grid=(B, S // T),
        in_specs=[
            tile,
            next_tile,
            _row_spec(D, l),
            _const_spec((None, D, 2 * D_FF), (l, 0, 0)),
            _const_spec((None, CONV_TAIL, 2 * D_FF), (l, 0, 0)),
            _const_spec((None, D_FF, D), (l, 0, 0)),
            _const_spec((1, D), (0, 0)),
        ],
        out_specs=tile,
        out_shape=jax.ShapeDtypeStruct(x.shape, F32),
        scratch_shapes=[
            pltpu.VMEM((2, T, D), BF16),
            pltpu.VMEM((T, D_FF), BF16),
            pltpu.VMEM((NB, CONV_TAIL, 2 * FF_BLOCK), F32),
        ],
        compiler_params=pltpu.CompilerParams(
            dimension_semantics=("parallel", "arbitrary"),
            vmem_limit_bytes=VMEM_LIMIT_BYTES),
        name=f"ffn_l{l}",
    )(x, x, ffn_norm_g, wup_b, conv_taps, wd_b, final_g)


def _conv_taps(ffn_conv_w, ffn_conv_b):
    taps = jnp.concatenate([ffn_conv_w, ffn_conv_b[:, None, :]], axis=1)
    return jnp.pad(taps, ((0, 0), (0, CONV_TAIL - taps.shape[1]), (0, 0)))


def kernel(x, mem, positions, mix_norm_g, w_in, ret_gn_g, pool_w, pool_scale, w_out, xa_norm_g,
           mem_norm_g, xa_wq, xa_wkv, xa_wo, ffn_norm_g, ffn_w_up, ffn_conv_w, ffn_conv_b,
           ffn_w_down, final_norm_g):
    L = w_in.shape[0]
    S = x.shape[1]
    assert S % MIX_TILE == 0 and S % XA_TILE == 0 and S % FFN_TILE == 0
    assert MIX_TILE % ROW_BLOCK == 0 and XA_TILE % ROW_BLOCK == 0 and ROW_BLOCK % CHUNK == 0
    w_in_b, w_out_b, pool_w_b = w_in.astype(BF16), w_out.astype(BF16), pool_w.astype(BF16)
    wq_b, wkv_b, wo_b = xa_wq.astype(BF16), xa_wkv.astype(BF16), xa_wo.astype(BF16)
    wup_b, wd_b = ffn_w_up.astype(BF16), ffn_w_down.astype(BF16)
    conv_taps = _conv_taps(ffn_conv_w, ffn_conv_b)
    final_g = final_norm_g.reshape(1, D_MODEL)
    mix_norm_g, ret_gn_g, pool_scale = _rows(mix_norm_g), _rows(ret_gn_g), _rows(pool_scale)
    xa_norm_g, ffn_norm_g = _rows(xa_norm_g), _rows(ffn_norm_g)

    cosf, sinf = _rope_tables(positions)
    kt, v = _memory_kv(mem, mem_norm_g, wkv_b)
    for l in range(L):
        x = _mixer(x, cosf, sinf, l, mix_norm_g, w_in_b, ret_gn_g, pool_w_b, pool_scale, w_out_b)
        x = _cross_attention(x, l, xa_norm_g, wq_b, kt, v, wo_b)
        x = _ffn(x, l, ffn_norm_g, wup_b, conv_taps, wd_b, final_g, final=(l == L - 1))
    return x
```

```python
import functools
import math

import jax
import jax.numpy as jnp
from jax import lax
from jax.experimental import pallas as pl
from jax.experimental.pallas import tpu as pltpu

F32 = jnp.float32
BF16 = jnp.bfloat16

D_MODEL = 1024
N_MEM = 256
RET_WIDTH = 512
RET_HEADS = 4
RET_HEAD_DIM = 128
POOL_WIDTH = 512
POOL_WINDOWS = (2, 4, 8, 16)
POOL_GROUP = 128
POOL_TAIL = 16
CHUNK = 128
ROPE_BASE = 10000.0
XA_HEADS = 4
XA_HEAD_DIM = 256
D_FF = 2816
FF_BLOCK = 256
N_FF_BLOCKS = D_FF // FF_BLOCK
CONV_TAIL = 8
KV_BATCHES = 4
EPS = 1e-6

ROW_BLOCK = 512
MIX_TILE = 1024
XA_TILE = 1024
FFN_TILE = 512
VMEM_LIMIT_BYTES = 56 * 1024 * 1024


def _rmsnorm(x, g):
    ms = jnp.mean(x * x, axis=-1, keepdims=True)
    return x * lax.rsqrt(ms + EPS) * g


def _dot(a, b):
    return jnp.dot(a, b, preferred_element_type=F32)


def _dot_nt(a, b):
    return lax.dot_general(a, b, (((1,), (1,)), ((), ())), preferred_element_type=F32)


def _dot_tn(a, b):
    return lax.dot_general(a, b, (((0,), (0,)), ((), ())), preferred_element_type=F32)


def _const_spec(shape, index):
    return pl.BlockSpec(shape, lambda b, s: index, pipeline_mode=pl.Buffered(1))


def _row_spec(n, l):
    return _const_spec((None, 1, n), (l, 0, 0))


def _rows(a):
    return a.reshape(a.shape[0], 1, a.shape[1])


def _rope_kernel(pos_ref, cos_ref, sin_ref):
    T = pos_ref.shape[1]
    lane = lax.broadcasted_iota(jnp.int32, (1, RET_HEAD_DIM), 1)
    half = RET_HEAD_DIM // 2
    low = lane < half
    j2 = (2 * jnp.where(low, lane, lane - half)).astype(F32)
    inv_freq = jnp.exp(-math.log(ROPE_BASE) * j2 / RET_HEAD_DIM)
    pos = jnp.where(low, pos_ref[0, 0:T // 2], pos_ref[0, T // 2:T])
    ang = pos * inv_freq
    cos = jnp.cos(ang)
    sin = jnp.sin(ang)
    cos_sw = pltpu.roll(cos, half, 1)
    sin_sw = pltpu.roll(sin, half, 1)
    cos_ref[0, 0:T // 2] = jnp.where(low, cos, cos_sw)
    cos_ref[0, T // 2:T] = jnp.where(low, cos_sw, cos)
    sin_ref[0, 0:T // 2] = jnp.where(low, -sin, sin_sw)
    sin_ref[0, T // 2:T] = jnp.where(low, -sin_sw, sin)


def _rope_tables(positions):
    B, S = positions.shape
    T = ROW_BLOCK
    pos = positions.astype(F32)[..., None]
    out = jax.ShapeDtypeStruct((B, S, RET_HEAD_DIM), F32)
    return pl.pallas_call(
        _rope_kernel,
        grid=(B, S // T),
        in_specs=[pl.BlockSpec((1, T, 1), lambda b, s: (b, s, 0))],
        out_specs=[pl.BlockSpec((1, T, RET_HEAD_DIM), lambda b, s: (b, s, 0))] * 2,
        out_shape=[out, out],
        name="rope_tables",
    )(pos)


def _kv_kernel(mem_ref, g_ref, wkv_ref, kt_ref, v_ref):
    mem_n = _rmsnorm(mem_ref[...], g_ref[...]).astype(BF16)
    k = _dot(mem_n, wkv_ref[:, :D_MODEL])
    v = _dot(mem_n, wkv_ref[:, D_MODEL:]).astype(BF16)
    for bb in range(KV_BATCHES):
        rows = slice(bb * N_MEM, (bb + 1) * N_MEM)
        kt_ref[bb] = k[rows].T.astype(BF16)
        v_ref[bb] = v[rows]


def _memory_kv(mem, mem_norm_g, wkv_b):
    L = wkv_b.shape[0]
    B = mem.shape[0]
    assert B % KV_BATCHES == 0
    return pl.pallas_call(
        _kv_kernel,
        grid=(L, B // KV_BATCHES),
        in_specs=[
            pl.BlockSpec((KV_BATCHES * N_MEM, D_MODEL), lambda l, b: (b, 0)),
            pl.BlockSpec((1, D_MODEL), lambda l, b: (0, 0)),
            pl.BlockSpec((None, D_MODEL, 2 * D_MODEL), lambda l, b: (l, 0, 0)),
        ],
        out_specs=[
            pl.BlockSpec((None, KV_BATCHES, D_MODEL, N_MEM), lambda l, b: (l, b, 0, 0)),
            pl.BlockSpec((None, KV_BATCHES, N_MEM, D_MODEL), lambda l, b: (l, b, 0, 0)),
        ],
        out_shape=[
            jax.ShapeDtypeStruct((L, B, D_MODEL, N_MEM), BF16),
            jax.ShapeDtypeStruct((L, B, N_MEM, D_MODEL), BF16),
        ],
        compiler_params=pltpu.CompilerParams(vmem_limit_bytes=VMEM_LIMIT_BYTES),
        name="memory_kv",
    )(mem.reshape(B * N_MEM, D_MODEL), mem_norm_g.reshape(1, D_MODEL), wkv_b)


def _rope(t, cosf, sinf):
    return t * cosf + pltpu.roll(t, RET_HEAD_DIM // 2, 1) * sinf


def _trailing_window_sum(ext, wlen):
    span = 1
    while span < wlen:
        ext = ext + pltpu.roll(ext, span, 0)
        span *= 2
    return ext


def _mix_kernel(x_ref, cos_ref, sin_ref, ng_ref, win_ref, gng_ref, pw_ref, ps_ref, wout_ref,
                o_ref, state_ref, ubuf_ref, ybuf_ref, *, T):
    s = pl.program_id(1)

    @pl.when(s == 0)
    def _():
        state_ref[...] = jnp.zeros_like(state_ref)
        ubuf_ref[0:POOL_TAIL, :] = jnp.zeros((POOL_TAIL, POOL_WIDTH), F32)

    R = RET_WIDTH
    row = lax.broadcasted_iota(jnp.int32, (CHUNK, CHUNK), 0)
    col = lax.broadcasted_iota(jnp.int32, (CHUNK, CHUNK), 1)
    diff = (row - col).astype(F32)
    rowf = row.astype(F32)
    k_scale = RET_HEAD_DIM ** -0.5
    decay_consts = []
    for hd in range(RET_HEADS):
        lg = math.log1p(-(2.0 ** (-5 - hd)))
        dmat = jnp.where(diff >= 0, jnp.exp(lg * jnp.maximum(diff, 0.0)), 0.0) * k_scale
        zeta = jnp.exp(lg * (CHUNK - 1 - rowf)) * k_scale
        xi = jnp.exp(lg * (rowf + 1.0))
        decay_consts.append((dmat, zeta, xi, math.exp(lg * CHUNK)))

    n_blocks = T // ROW_BLOCK
    n_chunks = ROW_BLOCK // CHUNK
    head_lanes = [slice(hd * RET_HEAD_DIM, (hd + 1) * RET_HEAD_DIM) for hd in range(RET_HEADS)]
    st = [dict() for _ in range(n_blocks)]

    def rows_of(i):
        return slice(i * ROW_BLOCK, (i + 1) * ROW_BLOCK)

    def chunk_rows(c):
        return slice(c * CHUNK, (c + 1) * CHUNK)

    def project_qk(i):
        b = st[i]
        b["h"] = _rmsnorm(x_ref[0, rows_of(i)], ng_ref[...]).astype(BF16)
        b["q"] = _dot(b["h"], win_ref[:, 0:R])
        b["k"] = _dot(b["h"], win_ref[:, R:2 * R])

    def project_rest(i):
        b = st[i]
        b["v"] = _dot(b["h"], win_ref[:, 2 * R:3 * R]).astype(BF16)
        b["g"] = _dot(b["h"], win_ref[:, 3 * R:4 * R])
        r0 = i * ROW_BLOCK
        ubuf_ref[POOL_TAIL + r0:POOL_TAIL + r0 + ROW_BLOCK, :] = _dot(
            b["h"], win_ref[:, 4 * R:4 * R + POOL_WIDTH])

    def retention_scores(i):
        b = st[i]
        cosf = cos_ref[0, rows_of(i)]
        sinf = sin_ref[0, rows_of(i)]
        b["qc"], b["scores"], b["kv"] = {}, {}, {}
        for hd, lanes in enumerate(head_lanes):
            zeta = decay_consts[hd][1]
            qr = _rope(b["q"][:, lanes], cosf, sinf).astype(BF16)
            kr = _rope(b["k"][:, lanes], cosf, sinf)
            for c in range(n_chunks):
                rows = chunk_rows(c)
                vc = b["v"][rows, lanes]
                b["qc"][hd, c] = qr[rows]
                b["scores"][hd, c] = _dot_nt(qr[rows], kr[rows].astype(BF16))
                b["kv"][hd, c] = _dot_tn((kr[rows] * zeta).astype(BF16), vc)

    def retention_states(i):
        b = st[i]
        b["p"], b["state_in"] = {}, {}
        for hd in range(RET_HEADS):
            dmat, _, _, chunk_decay = decay_consts[hd]
            state = state_ref[hd]
            for c in range(n_chunks):
                b["p"][hd, c] = (b["scores"][hd, c] * dmat).astype(BF16)
                b["state_in"][hd, c] = state.astype(BF16)
                state = state * chunk_decay + b["kv"][hd, c]
            state_ref[hd] = state

    def retention_outputs(i):
        b = st[i]
        r0 = i * ROW_BLOCK
        for hd, lanes in enumerate(head_lanes):
            xi = decay_consts[hd][2]
            gn = gng_ref[:, lanes]
            for c in range(n_chunks):
                rows = chunk_rows(c)
                intra = _dot(b["p"][hd, c], b["v"][rows, lanes])
                cross = _dot(b["qc"][hd, c], b["state_in"][hd, c]) * xi
                ret = intra + cross
                ms = jnp.mean(ret * ret, axis=-1, keepdims=True)
                yn = ret * lax.rsqrt(ms + EPS) * gn
                gc = b["g"][rows, lanes]
                ybuf_ref[r0 + c * CHUNK:r0 + (c + 1) * CHUNK, lanes] = (
                    gc * jax.nn.sigmoid(gc) * yn).astype(BF16)

    def pooling(i):
        r0 = i * ROW_BLOCK
        t_next = (s * T + r0 + 1 + lax.broadcasted_iota(jnp.int32, (ROW_BLOCK, POOL_GROUP), 0)
                  ).astype(F32)
        for gi, wlen in enumerate(POOL_WINDOWS):
            lanes = slice(gi * POOL_GROUP, (gi + 1) * POOL_GROUP)
            ext = ubuf_ref[r0:r0 + ROW_BLOCK + POOL_TAIL, lanes]
            wsum = _trailing_window_sum(ext, wlen)[POOL_TAIL:]
            inv_count = 1.0 / jnp.minimum(t_next, float(wlen))
            pooled = (wsum * inv_count - ext[POOL_TAIL:]).astype(BF16)
            og = _dot(pooled, pw_ref[gi]) * ps_ref[:, lanes]
            ybuf_ref[rows_of(i), R + gi * POOL_GROUP:R + (gi + 1) * POOL_GROUP] = og.astype(BF16)

    def output_pool_half(i):
        o_ref[0, rows_of(i)] = x_ref[0, rows_of(i)] + _dot(ybuf_ref[rows_of(i), R:], wout_ref[R:, :])

    def output_retention_half(i):
        o_ref[0, rows_of(i)] += _dot(ybuf_ref[rows_of(i), :R], wout_ref[:R, :])

    project_qk(0)
    project_rest(0)
    retention_scores(0)
    for i in range(n_blocks):
        has_next = i + 1 < n_blocks
        if has_next:
            project_qk(i + 1)
        retention_states(i)
        retention_outputs(i)
        pooling(i)
        if has_next:
            project_rest(i + 1)
            retention_scores(i + 1)
        if i > 0:
            output_retention_half(i - 1)
        output_pool_half(i)
    output_retention_half(n_blocks - 1)

    ubuf_ref[0:POOL_TAIL, :] = ubuf_ref[T:T + POOL_TAIL, :]


def _mixer(x, cosf, sinf, l, mix_norm_g, w_in_b, ret_gn_g, pool_w_b, pool_scale, w_out_b):
    B, S, D = x.shape
    T = MIX_TILE
    tile = pl.BlockSpec((1, T, D), lambda b, s: (b, s, 0))
    rope = pl.BlockSpec((1, T, RET_HEAD_DIM), lambda b, s: (b, s, 0))
    return pl.pallas_call(
        functools.partial(_mix_kernel, T=T),
        grid=(B, S // T),
        in_specs=[
            tile, rope, rope,
            _row_spec(D, l),
            _const_spec((None, D, w_in_b.shape[-1]), (l, 0, 0)),
            _row_spec(RET_WIDTH, l),
            _const_spec((None, len(POOL_WINDOWS), POOL_GROUP, POOL_GROUP), (l, 0, 0, 0)),
            _row_spec(POOL_WIDTH, l),
            _const_spec((None, D, D), (l, 0, 0)),
        ],
        out_specs=tile,
        out_shape=jax.ShapeDtypeStruct(x.shape, F32),
        scratch_shapes=[
            pltpu.VMEM((RET_HEADS, RET_HEAD_DIM, RET_HEAD_DIM), F32),
            pltpu.VMEM((T + POOL_TAIL, POOL_WIDTH), F32),
            pltpu.VMEM((T, D), BF16),
        ],
        compiler_params=pltpu.CompilerParams(
            dimension_semantics=("parallel", "arbitrary"),
            vmem_limit_bytes=VMEM_LIMIT_BYTES),
        name=f"mixer_l{l}",
    )(x, cosf, sinf, mix_norm_g, w_in_b, ret_gn_g, pool_w_b, pool_scale, w_out_b)


def _xa_kernel(x_ref, ng_ref, wq_ref, kt_ref, v_ref, wo_ref, o_ref, obuf_ref, *, T):
    n_blocks = T // ROW_BLOCK
    heads = [slice(hd * XA_HEAD_DIM, (hd + 1) * XA_HEAD_DIM) for hd in range(XA_HEADS)]
    scores = [None] * n_blocks

    def rows_of(i):
        return slice(i * ROW_BLOCK, (i + 1) * ROW_BLOCK)

    def score(i):
        h = _rmsnorm(x_ref[0, rows_of(i)], ng_ref[...]).astype(BF16)
        q = _dot(h, wq_ref[...])
        scores[i] = [_dot(q[:, dims].astype(BF16), kt_ref[0, dims, :]) * (XA_HEAD_DIM ** -0.5)
                     for dims in heads]

    def attend(i):
        for dims, sc in zip(heads, scores[i]):
            p = jnp.exp(sc - jnp.max(sc, axis=-1, keepdims=True))
            p = p * (1.0 / jnp.sum(p, axis=-1, keepdims=True))
            obuf_ref[rows_of(i), dims] = _dot(p.astype(BF16), v_ref[0, :, dims]).astype(BF16)

    def project_out(i):
        o_ref[0, rows_of(i)] = x_ref[0, rows_of(i)] + _dot(obuf_ref[rows_of(i), :], wo_ref[...])

    score(0)
    for i in range(n_blocks):
        if i + 1 < n_blocks:
            score(i + 1)
        attend(i)
        if i > 0:
            project_out(i - 1)
    project_out(n_blocks - 1)


def _cross_attention(x, l, xa_norm_g, wq_b, kt, v, wo_b):
    B, S, D = x.shape
    T = XA_TILE
    tile = pl.BlockSpec((1, T, D), lambda b, s: (b, s, 0))
    return pl.pallas_call(
        functools.partial(_xa_kernel, T=T),
        grid=(B, S // T),
        in_specs=[
            tile,
            _row_spec(D, l),
            _const_spec((None, D, D), (l, 0, 0)),
            pl.BlockSpec((None, 1, D, N_MEM), lambda b, s: (l, b, 0, 0)),
            pl.BlockSpec((None, 1, N_MEM, D), lambda b, s: (l, b, 0, 0)),
            _const_spec((None, D, D), (l, 0, 0)),
        ],
        out_specs=tile,
        out_shape=jax.ShapeDtypeStruct(x.shape, F32),
        scratch_shapes=[pltpu.VMEM((T, D), BF16)],
        compiler_params=pltpu.CompilerParams(
            dimension_semantics=("parallel", "arbitrary"),
            vmem_limit_bytes=VMEM_LIMIT_BYTES),
        name=f"cross_attn_l{l}",
    )(x, xa_norm_g, wq_b, kt, v, wo_b)


def _shift_rows(a, prev, k):
    rolled = pltpu.roll(a, k, 0)
    row = lax.broadcasted_iota(jnp.int32, prev.shape, 0)
    head = jnp.where(row < k, pltpu.roll(prev, k, 0), rolled[0:CONV_TAIL])
    return jnp.concatenate([head, rolled[CONV_TAIL:]], axis=0)


def _causal_conv(a, tail_ref, cw):
    prev = tail_ref[...]
    tail_ref[...] = a[a.shape[0] - CONV_TAIL:]
    return (cw[0:1] * _shift_rows(a, prev, 2) + cw[1:2] * _shift_rows(a, prev, 1)
            + cw[2:3] * a + cw[3:4])


def _ffn_kernel(x_ref, ng_ref, wup_ref, cw_ref, wd_ref, fg_ref, o_ref,
                hbuf_ref, act_ref, tail_ref, *, final):
    @pl.when(pl.program_id(1) == 0)
    def _():
        tail_ref[...] = jnp.zeros_like(tail_ref)

    hbuf_ref[...] = _rmsnorm(x_ref[0], ng_ref[...]).astype(BF16)

    for j in range(N_FF_BLOCKS):
        h = hbuf_ref[...]
        gate = slice(j * FF_BLOCK, (j + 1) * FF_BLOCK)
        val = slice(D_FF + j * FF_BLOCK, D_FF + (j + 1) * FF_BLOCK)
        cg = _causal_conv(_dot(h, wup_ref[:, gate]), tail_ref.at[j], cw_ref[:, gate])
        cv = _causal_conv(_dot(h, wup_ref[:, val]), tail_ref.at[N_FF_BLOCKS + j], cw_ref[:, val])
        act_ref[:, gate] = (cg * jax.nn.sigmoid(cg) * cv).astype(BF16)

    o_ref[0] = x_ref[0] + _dot(act_ref[...], wd_ref[...])
    if final:
        o_ref[0] = _rmsnorm(o_ref[0], fg_ref[...])


def _ffn(x, l, ffn_norm_g, wup_b, conv_taps, wd_b, final_g, final):
    B, S, D = x.shape
    T = FFN_TILE
    NB = N_FF_BLOCKS
    tile = pl.BlockSpec((1, T, D), lambda b, s: (b, s, 0))
    return pl.pallas_call(
        functools.partial(_ffn_kernel, final=final),
        grid=(B, S // T),
        in_specs=[
            tile,
            _row_spec(D, l),
            _const_spec((None, D, 2 * D_FF), (l, 0, 0)),
            _const_spec((None, CONV_TAIL, 2 * D_FF), (l, 0, 0)),
            _const_spec((None, D_FF, D), (l, 0, 0)),
            _const_spec((1, D), (0, 0)),
        ],
        out_specs=tile,
        out_shape=jax.ShapeDtypeStruct(x.shape, F32),
        scratch_shapes=[
            pltpu.VMEM((T, D), BF16),
            pltpu.VMEM((T, D_FF), BF16),
            pltpu.VMEM((2 * NB, CONV_TAIL, FF_BLOCK), F32),
        ],
        compiler_params=pltpu.CompilerParams(
            dimension_semantics=("parallel", "arbitrary"),
            vmem_limit_bytes=VMEM_LIMIT_BYTES),
        name=f"ffn_l{l}",
    )(x, ffn_norm_g, wup_b, conv_taps, wd_b, final_g)


def _conv_taps(ffn_conv_w, ffn_conv_b):
    taps = jnp.concatenate([ffn_conv_w, ffn_conv_b[:, None, :]], axis=1)
    return jnp.pad(taps, ((0, 0), (0, CONV_TAIL - taps.shape[1]), (0, 0)))


def kernel(x, mem, positions, mix_norm_g, w_in, ret_gn_g, pool_w, pool_scale, w_out, xa_norm_g,
           mem_norm_g, xa_wq, xa_wkv, xa_wo, ffn_norm_g, ffn_w_up, ffn_conv_w, ffn_conv_b,
           ffn_w_down, final_norm_g):
    L = w_in.shape[0]
    S = x.shape[1]
    assert S % MIX_TILE == 0 and S % XA_TILE == 0 and S % FFN_TILE == 0
    assert MIX_TILE % ROW_BLOCK == 0 and XA_TILE % ROW_BLOCK == 0 and ROW_BLOCK % CHUNK == 0
    w_in_b, w_out_b, pool_w_b = w_in.astype(BF16), w_out.astype(BF16), pool_w.astype(BF16)
    wq_b, wkv_b, wo_b = xa_wq.astype(BF16), xa_wkv.astype(BF16), xa_wo.astype(BF16)
    wup_b, wd_b = ffn_w_up.astype(BF16), ffn_w_down.astype(BF16)
    conv_taps = _conv_taps(ffn_conv_w, ffn_conv_b)
    final_g = final_norm_g.reshape(1, D_MODEL)
    mix_norm_g, ret_gn_g, pool_scale = _rows(mix_norm_g), _rows(ret_gn_g), _rows(pool_scale)
    xa_norm_g, ffn_norm_g = _rows(xa_norm_g), _rows(ffn_norm_g)

    cosf, sinf = _rope_tables(positions)
    kt, v = _memory_kv(mem, mem_norm_g, wkv_b)
    for l in range(L):
        x = _mixer(x, cosf, sinf, l, mix_norm_g, w_in_b, ret_gn_g, pool_w_b, pool_scale, w_out_b)
        x = _cross_attention(x, l, xa_norm_g, wq_b, kt, v, wo_b)
        x = _ffn(x, l, ffn_norm_g, wup_b, conv_taps, wd_b, final_g, final=(l == L - 1))
    return x
```

```python
import functools
import math

import jax
import jax.numpy as jnp
from jax import lax
from jax.experimental import pallas as pl
from jax.experimental.pallas import tpu as pltpu

F32 = jnp.float32
BF16 = jnp.bfloat16

D_MODEL = 1024
N_MEM = 256
RET_WIDTH = 512
RET_HEADS = 4
RET_HEAD_DIM = 128
POOL_WIDTH = 512
POOL_WINDOWS = (2, 4, 8, 16)
POOL_GROUP = 128
POOL_TAIL = 16
CHUNK = 128
ROPE_BASE = 10000.0
XA_HEADS = 4
XA_HEAD_DIM = 256
D_FF = 2816
FF_BLOCK = 256
N_FF_BLOCKS = D_FF // FF_BLOCK
CONV_TAIL = 8
KV_BATCHES = 4
EPS = 1e-6

ROW_BLOCK = 512
MIX_TILE = 1024
XA_TILE = 1024
FFN_TILE = 512
VMEM_LIMIT_BYTES = 56 * 1024 * 1024


def _rmsnorm(x, g):
    ms = jnp.mean(x * x, axis=-1, keepdims=True)
    return x * lax.rsqrt(ms + EPS) * g


def _dot(a, b):
    return jnp.dot(a, b, preferred_element_type=F32)


def _dot_nt(a, b):
    return lax.dot_general(a, b, (((1,), (1,)), ((), ())), preferred_element_type=F32)


def _dot_tn(a, b):
    return lax.dot_general(a, b, (((0,), (0,)), ((), ())), preferred_element_type=F32)


def _const_spec(shape, index):
    return pl.BlockSpec(shape, lambda b, s: index, pipeline_mode=pl.Buffered(1))


def _row_spec(n, l):
    return _const_spec((None, 1, n), (l, 0, 0))


def _rows(a):
    return a.reshape(a.shape[0], 1, a.shape[1])


def _rope_kernel(pos_ref, cos_ref, sin_ref):
    T = pos_ref.shape[1]
    lane = lax.broadcasted_iota(jnp.int32, (1, RET_HEAD_DIM), 1)
    half = RET_HEAD_DIM // 2
    low = lane < half
    j2 = (2 * jnp.where(low, lane, lane - half)).astype(F32)
    inv_freq = jnp.exp(-math.log(ROPE_BASE) * j2 / RET_HEAD_DIM)
    pos = jnp.where(low, pos_ref[0, 0:T // 2], pos_ref[0, T // 2:T])
    ang = pos * inv_freq
    cos = jnp.cos(ang)
    sin = jnp.sin(ang)
    cos_sw = pltpu.roll(cos, half, 1)
    sin_sw = pltpu.roll(sin, half, 1)
    cos_ref[0, 0:T // 2] = jnp.where(low, cos, cos_sw)
    cos_ref[0, T // 2:T] = jnp.where(low, cos_sw, cos)
    sin_ref[0, 0:T // 2] = jnp.where(low, -sin, sin_sw)
    sin_ref[0, T // 2:T] = jnp.where(low, -sin_sw, sin)


def _rope_tables(positions):
    B, S = positions.shape
    T = ROW_BLOCK
    pos = positions.astype(F32)[..., None]
    out = jax.ShapeDtypeStruct((B, S, RET_HEAD_DIM), F32)
    return pl.pallas_call(
        _rope_kernel,
        grid=(B, S // T),
        in_specs=[pl.BlockSpec((1, T, 1), lambda b, s: (b, s, 0))],
        out_specs=[pl.BlockSpec((1, T, RET_HEAD_DIM), lambda b, s: (b, s, 0))] * 2,
        out_shape=[out, out],
        name="rope_tables",
    )(pos)


def _kv_kernel(mem_ref, g_ref, wkv_ref, kt_ref, v_ref):
    mem_n = _rmsnorm(mem_ref[...], g_ref[...]).astype(BF16)
    k = _dot(mem_n, wkv_ref[:, :D_MODEL])
    v = _dot(mem_n, wkv_ref[:, D_MODEL:]).astype(BF16)
    for bb in range(KV_BATCHES):
        rows = slice(bb * N_MEM, (bb + 1) * N_MEM)
        kt_ref[bb] = k[rows].T.astype(BF16)
        v_ref[bb] = v[rows]


def _memory_kv(mem, mem_norm_g, wkv_b):
    L = wkv_b.shape[0]
    B = mem.shape[0]
    assert B % KV_BATCHES == 0
    return pl.pallas_call(
        _kv_kernel,
        grid=(L, B // KV_BATCHES),
        in_specs=[
            pl.BlockSpec((KV_BATCHES * N_MEM, D_MODEL), lambda l, b: (b, 0)),
            pl.BlockSpec((1, D_MODEL), lambda l, b: (0, 0)),
            pl.BlockSpec((None, D_MODEL, 2 * D_MODEL), lambda l, b: (l, 0, 0)),
        ],
        out_specs=[
            pl.BlockSpec((None, KV_BATCHES, D_MODEL, N_MEM), lambda l, b: (l, b, 0, 0)),
            pl.BlockSpec((None, KV_BATCHES, N_MEM, D_MODEL), lambda l, b: (l, b, 0, 0)),
        ],
        out_shape=[
            jax.ShapeDtypeStruct((L, B, D_MODEL, N_MEM), BF16),
            jax.ShapeDtypeStruct((L, B, N_MEM, D_MODEL), BF16),
        ],
        compiler_params=pltpu.CompilerParams(vmem_limit_bytes=VMEM_LIMIT_BYTES),
        name="memory_kv",
    )(mem.reshape(B * N_MEM, D_MODEL), mem_norm_g.reshape(1, D_MODEL), wkv_b)


def _rope(t, cosf, sinf):
    return t * cosf + pltpu.roll(t, RET_HEAD_DIM // 2, 1) * sinf


def _trailing_window_sum(ext, wlen):
    span = 1
    while span < wlen:
        ext = ext + pltpu.roll(ext, span, 0)
        span *= 2
    return ext


def _mix_kernel(x_ref, cos_ref, sin_ref, ng_ref, win_ref, gng_ref, pw_ref, ps_ref, wout_ref,
                o_ref, state_ref, ubuf_ref, ybuf_ref, *, T):
    s = pl.program_id(1)

    @pl.when(s == 0)
    def _():
        state_ref[...] = jnp.zeros_like(state_ref)
        ubuf_ref[0:POOL_TAIL, :] = jnp.zeros((POOL_TAIL, POOL_WIDTH), F32)

    R = RET_WIDTH
    row = lax.broadcasted_iota(jnp.int32, (CHUNK, CHUNK), 0)
    col = lax.broadcasted_iota(jnp.int32, (CHUNK, CHUNK), 1)
    diff = (row - col).astype(F32)
    rowf = row.astype(F32)
    k_scale = RET_HEAD_DIM ** -0.5
    decay_consts = []
    for hd in range(RET_HEADS):
        lg = math.log1p(-(2.0 ** (-5 - hd)))
        dmat = jnp.where(diff >= 0, jnp.exp(lg * jnp.maximum(diff, 0.0)), 0.0) * k_scale
        zeta = jnp.exp(lg * (CHUNK - 1 - rowf)) * k_scale
        xi = jnp.exp(lg * (rowf + 1.0))
        decay_consts.append((dmat, zeta, xi, math.exp(lg * CHUNK)))

    n_blocks = T // ROW_BLOCK
    n_chunks = ROW_BLOCK // CHUNK
    head_lanes = [slice(hd * RET_HEAD_DIM, (hd + 1) * RET_HEAD_DIM) for hd in range(RET_HEADS)]
    st = [dict() for _ in range(n_blocks)]

    def rows_of(i):
        return slice(i * ROW_BLOCK, (i + 1) * ROW_BLOCK)

    def chunk_rows(c):
        return slice(c * CHUNK, (c + 1) * CHUNK)

    def project_qk(i):
        b = st[i]
        b["h"] = _rmsnorm(x_ref[0, rows_of(i)], ng_ref[...]).astype(BF16)
        b["q"] = _dot(b["h"], win_ref[:, 0:R])
        b["k"] = _dot(b["h"], win_ref[:, R:2 * R])

    def project_rest(i):
        b = st[i]
        b["v"] = _dot(b["h"], win_ref[:, 2 * R:3 * R]).astype(BF16)
        b["g"] = _dot(b["h"], win_ref[:, 3 * R:4 * R])
        r0 = i * ROW_BLOCK
        ubuf_ref[POOL_TAIL + r0:POOL_TAIL + r0 + ROW_BLOCK, :] = _dot(
            b["h"], win_ref[:, 4 * R:4 * R + POOL_WIDTH])

    def retention_scores(i):
        b = st[i]
        cosf = cos_ref[0, rows_of(i)]
        sinf = sin_ref[0, rows_of(i)]
        b["qc"], b["scores"], b["kv"] = {}, {}, {}
        for hd, lanes in enumerate(head_lanes):
            zeta = decay_consts[hd][1]
            qr = _rope(b["q"][:, lanes], cosf, sinf).astype(BF16)
            kr = _rope(b["k"][:, lanes], cosf, sinf)
            for c in range(n_chunks):
                rows = chunk_rows(c)
                vc = b["v"][rows, lanes]
                b["qc"][hd, c] = qr[rows]
                b["scores"][hd, c] = _dot_nt(qr[rows], kr[rows].astype(BF16))
                b["kv"][hd, c] = _dot_tn((kr[rows] * zeta).astype(BF16), vc)

    def retention_states(i):
        b = st[i]
        b["p"], b["state_in"] = {}, {}
        for hd in range(RET_HEADS):
            dmat, _, _, chunk_decay = decay_consts[hd]
            state = state_ref[hd]
            for c in range(n_chunks):
                b["p"][hd, c] = (b["scores"][hd, c] * dmat).astype(BF16)
                b["state_in"][hd, c] = state.astype(BF16)
                state = state * chunk_decay + b["kv"][hd, c]
            state_ref[hd] = state

    def retention_outputs(i):
        b = st[i]
        r0 = i * ROW_BLOCK
        for hd, lanes in enumerate(head_lanes):
            xi = decay_consts[hd][2]
            gn = gng_ref[:, lanes]
            for c in range(n_chunks):
                rows = chunk_rows(c)
                intra = _dot(b["p"][hd, c], b["v"][rows, lanes])
                cross = _dot(b["qc"][hd, c], b["state_in"][hd, c]) * xi
                ret = intra + cross
                ms = jnp.mean(ret * ret, axis=-1, keepdims=True)
                yn = ret * lax.rsqrt(ms + EPS) * gn
                gc = b["g"][rows, lanes]
                ybuf_ref[r0 + c * CHUNK:r0 + (c + 1) * CHUNK, lanes] = (
                    gc * jax.nn.sigmoid(gc) * yn).astype(BF16)

    def pooling(i):
        r0 = i * ROW_BLOCK
        t_next = (s * T + r0 + 1 + lax.broadcasted_iota(jnp.int32, (ROW_BLOCK, POOL_GROUP), 0)
                  ).astype(F32)
        for gi, wlen in enumerate(POOL_WINDOWS):
            lanes = slice(gi * POOL_GROUP, (gi + 1) * POOL_GROUP)
            ext = ubuf_ref[r0:r0 + ROW_BLOCK + POOL_TAIL, lanes]
            wsum = _trailing_window_sum(ext, wlen)[POOL_TAIL:]
            inv_count = 1.0 / jnp.minimum(t_next, float(wlen))
            pooled = (wsum * inv_count - ext[POOL_TAIL:]).astype(BF16)
            og = _dot(pooled, pw_ref[gi]) * ps_ref[:, lanes]
            ybuf_ref[rows_of(i), R + gi * POOL_GROUP:R + (gi + 1) * POOL_GROUP] = og.astype(BF16)

    def output_pool_half(i):
        o_ref[0, rows_of(i)] = x_ref[0, rows_of(i)] + _dot(ybuf_ref[rows_of(i), R:], wout_ref[R:, :])

    def output_retention_half(i):
        o_ref[0, rows_of(i)] += _dot(ybuf_ref[rows_of(i), :R], wout_ref[:R, :])

    project_qk(0)
    project_rest(0)
    retention_scores(0)
    for i in range(n_blocks):
        has_next = i + 1 < n_blocks
        if has_next:
            project_qk(i + 1)
        pooling(i)
        retention_states(i)
        retention_outputs(i)
        if has_next:
            project_rest(i + 1)
            retention_scores(i + 1)
        if i > 0:
            output_retention_half(i - 1)
        output_pool_half(i)
    output_retention_half(n_blocks - 1)

    ubuf_ref[0:POOL_TAIL, :] = ubuf_ref[T:T + POOL_TAIL, :]


def _mixer(x, cosf, sinf, l, mix_norm_g, w_in_b, ret_gn_g, pool_w_b, pool_scale, w_out_b):
    B, S, D = x.shape
    T = MIX_TILE
    tile = pl.BlockSpec((1, T, D), lambda b, s: (b, s, 0))
    rope = pl.BlockSpec((1, T, RET_HEAD_DIM), lambda b, s: (b, s, 0))
    return pl.pallas_call(
        functools.partial(_mix_kernel, T=T),
        grid=(B, S // T),
        in_specs=[
            tile, rope, rope,
            _row_spec(D, l),
            _const_spec((None, D, w_in_b.shape[-1]), (l, 0, 0)),
            _row_spec(RET_WIDTH, l),
            _const_spec((None, len(POOL_WINDOWS), POOL_GROUP, POOL_GROUP), (l, 0, 0, 0)),
            _row_spec(POOL_WIDTH, l),
            _const_spec((None, D, D), (l, 0, 0)),
        ],
        out_specs=tile,
        out_shape=jax.ShapeDtypeStruct(x.shape, F32),
        scratch_shapes=[
            pltpu.VMEM((RET_HEADS, RET_HEAD_DIM, RET_HEAD_DIM), F32),
            pltpu.VMEM((T + POOL_TAIL, POOL_WIDTH), F32),
            pltpu.VMEM((T, D), BF16),
        ],
        compiler_params=pltpu.CompilerParams(
            dimension_semantics=("parallel", "arbitrary"),
            vmem_limit_bytes=VMEM_LIMIT_BYTES),
        name=f"mixer_l{l}",
    )(x, cosf, sinf, mix_norm_g, w_in_b, ret_gn_g, pool_w_b, pool_scale, w_out_b)


def _xa_kernel(x_ref, ng_ref, wq_ref, kt_ref, v_ref, wo_ref, o_ref, obuf_ref, *, T):
    n_blocks = T // ROW_BLOCK
    heads = [slice(hd * XA_HEAD_DIM, (hd + 1) * XA_HEAD_DIM) for hd in range(XA_HEADS)]
    scores = [None] * n_blocks

    def rows_of(i):
        return slice(i * ROW_BLOCK, (i + 1) * ROW_BLOCK)

    def score(i):
        h = _rmsnorm(x_ref[0, rows_of(i)], ng_ref[...]).astype(BF16)
        q = _dot(h, wq_ref[...])
        scores[i] = [_dot(q[:, dims].astype(BF16), kt_ref[0, dims, :]) * (XA_HEAD_DIM ** -0.5)
                     for dims in heads]

    def attend(i):
        for dims, sc in zip(heads, scores[i]):
            p = jnp.exp(sc - jnp.max(sc, axis=-1, keepdims=True))
            p = p * (1.0 / jnp.sum(p, axis=-1, keepdims=True))
            obuf_ref[rows_of(i), dims] = _dot(p.astype(BF16), v_ref[0, :, dims]).astype(BF16)

    def project_out(i):
        o_ref[0, rows_of(i)] = x_ref[0, rows_of(i)] + _dot(obuf_ref[rows_of(i), :], wo_ref[...])

    score(0)
    for i in range(n_blocks):
        if i + 1 < n_blocks:
            score(i + 1)
        attend(i)
        project_out(i)


def _cross_attention(x, l, xa_norm_g, wq_b, kt, v, wo_b):
    B, S, D = x.shape
    T = XA_TILE
    tile = pl.BlockSpec((1, T, D), lambda b, s: (b, s, 0))
    return pl.pallas_call(
        functools.partial(_xa_kernel, T=T),
        grid=(B, S // T),
        in_specs=[
            tile,
            _row_spec(D, l),
            _const_spec((None, D, D), (l, 0, 0)),
            pl.BlockSpec((None, 1, D, N_MEM), lambda b, s: (l, b, 0, 0)),
            pl.BlockSpec((None, 1, N_MEM, D), lambda b, s: (l, b, 0, 0)),
            _const_spec((None, D, D), (l, 0, 0)),
        ],
        out_specs=tile,
        out_shape=jax.ShapeDtypeStruct(x.shape, F32),
        scratch_shapes=[pltpu.VMEM((T, D), BF16)],
        compiler_params=pltpu.CompilerParams(
            dimension_semantics=("parallel", "arbitrary"),
            vmem_limit_bytes=VMEM_LIMIT_BYTES),
        name=f"cross_attn_l{l}",
    )(x, xa_norm_g, wq_b, kt, v, wo_b)


def _shift_rows(a, prev, k):
    rolled = pltpu.roll(a, k, 0)
    row = lax.broadcasted_iota(jnp.int32, prev.shape, 0)
    head = jnp.where(row < k, pltpu.roll(prev, k, 0), rolled[0:CONV_TAIL])
    return jnp.concatenate([head, rolled[CONV_TAIL:]], axis=0)


def _causal_conv(a, tail_ref, cw):
    prev = tail_ref[...]
    tail_ref[...] = a[a.shape[0] - CONV_TAIL:]
    return (cw[0:1] * _shift_rows(a, prev, 2) + cw[1:2] * _shift_rows(a, prev, 1)
            + cw[2:3] * a + cw[3:4])


def _ffn_kernel(x_ref, ng_ref, wup_ref, cw_ref, wd_ref, fg_ref, o_ref,
                hbuf_ref, act_ref, tail_ref, *, final):
    @pl.when(pl.program_id(1) == 0)
    def _():
        tail_ref[...] = jnp.zeros_like(tail_ref)

    hbuf_ref[...] = _rmsnorm(x_ref[0], ng_ref[...]).astype(BF16)

    for j in range(N_FF_BLOCKS):
        h = hbuf_ref[...]
        gate = slice(j * FF_BLOCK, (j + 1) * FF_BLOCK)
        val = slice(D_FF + j * FF_BLOCK, D_FF + (j + 1) * FF_BLOCK)
        cg = _causal_conv(_dot(h, wup_ref[:, gate]), tail_ref.at[j], cw_ref[:, gate])
        cv = _causal_conv(_dot(h, wup_ref[:, val]), tail_ref.at[N_FF_BLOCKS + j], cw_ref[:, val])
        act_ref[:, gate] = (cg * jax.nn.sigmoid(cg) * cv).astype(BF16)

    o_ref[0] = x_ref[0] + _dot(act_ref[...], wd_ref[...])
    if final:
        o_ref[0] = _rmsnorm(o_ref[0], fg_ref[...])


def _ffn(x, l, ffn_norm_g, wup_b, conv_taps, wd_b, final_g, final):
    B, S, D = x.shape
    T = FFN_TILE
    NB = N_FF_BLOCKS
    tile = pl.BlockSpec((1, T, D), lambda b, s: (b, s, 0))
    return pl.pallas_call(
        functools.partial(_ffn_kernel, final=final),
        grid=(B, S // T),
        in_specs=[
            tile,
            _row_spec(D, l),
            _const_spec((None, D, 2 * D_FF), (l, 0, 0)),
            _const_spec((None, CONV_TAIL, 2 * D_FF), (l, 0, 0)),
            _const_spec((None, D_FF, D), (l, 0, 0)),
            _const_spec((1, D), (0, 0)),
        ],
        out_specs=tile,
        out_shape=jax.ShapeDtypeStruct(x.shape, F32),
        scratch_shapes=[
            pltpu.VMEM((T, D), BF16),
            pltpu.VMEM((T, D_FF), BF16),
            pltpu.VMEM((2 * NB, CONV_TAIL, FF_BLOCK), F32),
        ],
        compiler_params=pltpu.CompilerParams(
            dimension_semantics=("parallel", "arbitrary"),
            vmem_limit_bytes=VMEM_LIMIT_BYTES),
        name=f"ffn_l{l}",
    )(x, ffn_norm_g, wup_b, conv_taps, wd_b, final_g)


def _conv_taps(ffn_conv_w, ffn_conv_b):
    taps = jnp.concatenate([ffn_conv_w, ffn_conv_b[:, None, :]], axis=1)
    return jnp.pad(taps, ((0, 0), (0, CONV_TAIL - taps.shape[1]), (0, 0)))


def kernel(x, mem, positions, mix_norm_g, w_in, ret_gn_g, pool_w, pool_scale, w_out, xa_norm_g,
           mem_norm_g, xa_wq, xa_wkv, xa_wo, ffn_norm_g, ffn_w_up, ffn_conv_w, ffn_conv_b,
           ffn_w_down, final_norm_g):
    L = w_in.shape[0]
    S = x.shape[1]
    assert S % MIX_TILE == 0 and S % XA_TILE == 0 and S % FFN_TILE == 0
    assert MIX_TILE % ROW_BLOCK == 0 and XA_TILE % ROW_BLOCK == 0 and ROW_BLOCK % CHUNK == 0
    w_in_b, w_out_b, pool_w_b = w_in.astype(BF16), w_out.astype(BF16), pool_w.astype(BF16)
    wq_b, wkv_b, wo_b = xa_wq.astype(BF16), xa_wkv.astype(BF16), xa_wo.astype(BF16)
    wup_b, wd_b = ffn_w_up.astype(BF16), ffn_w_down.astype(BF16)
    conv_taps = _conv_taps(ffn_conv_w, ffn_conv_b)
    final_g = final_norm_g.reshape(1, D_MODEL)
    mix_norm_g, ret_gn_g, pool_scale = _rows(mix_norm_g), _rows(ret_gn_g), _rows(pool_scale)
    xa_norm_g, ffn_norm_g = _rows(xa_norm_g), _rows(ffn_norm_g)

    cosf, sinf = _rope_tables(positions)
    kt, v = _memory_kv(mem, mem_norm_g, wkv_b)
    for l in range(L):
        x = _mixer(x, cosf, sinf, l, mix_norm_g, w_in_b, ret_gn_g, pool_w_b, pool_scale, w_out_b)
        x = _cross_attention(x, l, xa_norm_g, wq_b, kt, v, wo_b)
        x = _ffn(x, l, ffn_norm_g, wup_b, conv_taps, wd_b, final_g, final=(l == L - 1))
    return x
```

```python
import functools
import math

import jax
import jax.numpy as jnp
from jax import lax
from jax.experimental import pallas as pl
from jax.experimental.pallas import tpu as pltpu

F32 = jnp.float32
BF16 = jnp.bfloat16

D_MODEL = 1024
N_MEM = 256
RET_WIDTH = 512
RET_HEADS = 4
RET_HEAD_DIM = 128
POOL_WIDTH = 512
POOL_WINDOWS = (2, 4, 8, 16)
POOL_GROUP = 128
POOL_TAIL = 16
CHUNK = 128
ROPE_BASE = 10000.0
XA_HEADS = 4
XA_HEAD_DIM = 256
D_FF = 2816
FF_BLOCK = 256
N_FF_BLOCKS = D_FF // FF_BLOCK
CONV_TAIL = 8
KV_BATCHES = 4
EPS = 1e-6

ROW_BLOCK = 512
MIX_TILE = 1024
XA_TILE = 1024
FFN_TILE = 1024
VMEM_LIMIT_BYTES = 56 * 1024 * 1024


def _rmsnorm(x, g):
    ms = jnp.mean(x * x, axis=-1, keepdims=True)
    return x * lax.rsqrt(ms + EPS) * g


def _dot(a, b):
    return jnp.dot(a, b, preferred_element_type=F32)


def _dot_nt(a, b):
    return lax.dot_general(a, b, (((1,), (1,)), ((), ())), preferred_element_type=F32)


def _dot_tn(a, b):
    return lax.dot_general(a, b, (((0,), (0,)), ((), ())), preferred_element_type=F32)


def _const_spec(shape, index):
    return pl.BlockSpec(shape, lambda b, s: index, pipeline_mode=pl.Buffered(1))


def _row_spec(n, l):
    return _const_spec((None, 1, n), (l, 0, 0))


def _rows(a):
    return a.reshape(a.shape[0], 1, a.shape[1])


def _rope_kernel(pos_ref, cos_ref, sin_ref):
    T = pos_ref.shape[1]
    lane = lax.broadcasted_iota(jnp.int32, (1, RET_HEAD_DIM), 1)
    half = RET_HEAD_DIM // 2
    low = lane < half
    j2 = (2 * jnp.where(low, lane, lane - half)).astype(F32)
    inv_freq = jnp.exp(-math.log(ROPE_BASE) * j2 / RET_HEAD_DIM)
    pos = jnp.where(low, pos_ref[0, 0:T // 2], pos_ref[0, T // 2:T])
    ang = pos * inv_freq
    cos = jnp.cos(ang)
    sin = jnp.sin(ang)
    cos_sw = pltpu.roll(cos, half, 1)
    sin_sw = pltpu.roll(sin, half, 1)
    cos_ref[0, 0:T // 2] = jnp.where(low, cos, cos_sw)
    cos_ref[0, T // 2:T] = jnp.where(low, cos_sw, cos)
    sin_ref[0, 0:T // 2] = jnp.where(low, -sin, sin_sw)
    sin_ref[0, T // 2:T] = jnp.where(low, -sin_sw, sin)


def _rope_tables(positions):
    B, S = positions.shape
    T = ROW_BLOCK
    pos = positions.astype(F32)[..., None]
    out = jax.ShapeDtypeStruct((B, S, RET_HEAD_DIM), F32)
    return pl.pallas_call(
        _rope_kernel,
        grid=(B, S // T),
        in_specs=[pl.BlockSpec((1, T, 1), lambda b, s: (b, s, 0))],
        out_specs=[pl.BlockSpec((1, T, RET_HEAD_DIM), lambda b, s: (b, s, 0))] * 2,
        out_shape=[out, out],
        name="rope_tables",
    )(pos)


def _kv_kernel(mem_ref, g_ref, wkv_ref, kt_ref, v_ref):
    mem_n = _rmsnorm(mem_ref[...], g_ref[...]).astype(BF16)
    k = _dot(mem_n, wkv_ref[:, :D_MODEL])
    v = _dot(mem_n, wkv_ref[:, D_MODEL:]).astype(BF16)
    for bb in range(KV_BATCHES):
        rows = slice(bb * N_MEM, (bb + 1) * N_MEM)
        kt_ref[bb] = k[rows].T.astype(BF16)
        v_ref[bb] = v[rows]


def _memory_kv(mem, mem_norm_g, wkv_b):
    L = wkv_b.shape[0]
    B = mem.shape[0]
    assert B % KV_BATCHES == 0
    return pl.pallas_call(
        _kv_kernel,
        grid=(L, B // KV_BATCHES),
        in_specs=[
            pl.BlockSpec((KV_BATCHES * N_MEM, D_MODEL), lambda l, b: (b, 0)),
            pl.BlockSpec((1, D_MODEL), lambda l, b: (0, 0)),
            pl.BlockSpec((None, D_MODEL, 2 * D_MODEL), lambda l, b: (l, 0, 0)),
        ],
        out_specs=[
            pl.BlockSpec((None, KV_BATCHES, D_MODEL, N_MEM), lambda l, b: (l, b, 0, 0)),
            pl.BlockSpec((None, KV_BATCHES, N_MEM, D_MODEL), lambda l, b: (l, b, 0, 0)),
        ],
        out_shape=[
            jax.ShapeDtypeStruct((L, B, D_MODEL, N_MEM), BF16),
            jax.ShapeDtypeStruct((L, B, N_MEM, D_MODEL), BF16),
        ],
        compiler_params=pltpu.CompilerParams(vmem_limit_bytes=VMEM_LIMIT_BYTES),
        name="memory_kv",
    )(mem.reshape(B * N_MEM, D_MODEL), mem_norm_g.reshape(1, D_MODEL), wkv_b)


def _rope(t, cosf, sinf):
    return t * cosf + pltpu.roll(t, RET_HEAD_DIM // 2, 1) * sinf


def _trailing_window_sum(ext, wlen):
    span = 1
    while span < wlen:
        ext = ext + pltpu.roll(ext, span, 0)
        span *= 2
    return ext


def _mix_kernel(x_ref, cos_ref, sin_ref, ng_ref, win_ref, gng_ref, pw_ref, ps_ref, wout_ref,
                o_ref, state_ref, ubuf_ref, ybuf_ref, *, T):
    s = pl.program_id(1)

    @pl.when(s == 0)
    def _():
        state_ref[...] = jnp.zeros_like(state_ref)
        ubuf_ref[0:POOL_TAIL, :] = jnp.zeros((POOL_TAIL, POOL_WIDTH), F32)

    R = RET_WIDTH
    row = lax.broadcasted_iota(jnp.int32, (CHUNK, CHUNK), 0)
    col = lax.broadcasted_iota(jnp.int32, (CHUNK, CHUNK), 1)
    diff = (row - col).astype(F32)
    rowf = row.astype(F32)
    k_scale = RET_HEAD_DIM ** -0.5
    decay_consts = []
    for hd in range(RET_HEADS):
        lg = math.log1p(-(2.0 ** (-5 - hd)))
        dmat = jnp.where(diff >= 0, jnp.exp(lg * jnp.maximum(diff, 0.0)), 0.0) * k_scale
        zeta = jnp.exp(lg * (CHUNK - 1 - rowf)) * k_scale
        xi = jnp.exp(lg * (rowf + 1.0))
        decay_consts.append((dmat, zeta, xi, math.exp(lg * CHUNK)))

    n_blocks = T // ROW_BLOCK
    n_chunks = ROW_BLOCK // CHUNK
    head_lanes = [slice(hd * RET_HEAD_DIM, (hd + 1) * RET_HEAD_DIM) for hd in range(RET_HEADS)]
    st = [dict() for _ in range(n_blocks)]

    def rows_of(i):
        return slice(i * ROW_BLOCK, (i + 1) * ROW_BLOCK)

    def chunk_rows(c):
        return slice(c * CHUNK, (c + 1) * CHUNK)

    def project_qk(i):
        b = st[i]
        b["h"] = _rmsnorm(x_ref[0, rows_of(i)], ng_ref[...]).astype(BF16)
        b["q"] = _dot(b["h"], win_ref[:, 0:R])
        b["k"] = _dot(b["h"], win_ref[:, R:2 * R])

    def project_rest(i):
        b = st[i]
        b["v"] = _dot(b["h"], win_ref[:, 2 * R:3 * R]).astype(BF16)
        b["g"] = _dot(b["h"], win_ref[:, 3 * R:4 * R])
        r0 = i * ROW_BLOCK
        ubuf_ref[POOL_TAIL + r0:POOL_TAIL + r0 + ROW_BLOCK, :] = _dot(
            b["h"], win_ref[:, 4 * R:4 * R + POOL_WIDTH])

    def retention_scores(i):
        b = st[i]
        cosf = cos_ref[0, rows_of(i)]
        sinf = sin_ref[0, rows_of(i)]
        b["qc"], b["scores"], b["kv"] = {}, {}, {}
        for hd, lanes in enumerate(head_lanes):
            zeta = decay_consts[hd][1]
            qr = _rope(b["q"][:, lanes], cosf, sinf).astype(BF16)
            kr = _rope(b["k"][:, lanes], cosf, sinf)
            for c in range(n_chunks):
                rows = chunk_rows(c)
                vc = b["v"][rows, lanes]
                b["qc"][hd, c] = qr[rows]
                b["scores"][hd, c] = _dot_nt(qr[rows], kr[rows].astype(BF16))
                b["kv"][hd, c] = _dot_tn((kr[rows] * zeta).astype(BF16), vc)

    def retention_states(i):
        b = st[i]
        b["p"], b["state_in"] = {}, {}
        for hd in range(RET_HEADS):
            dmat, _, _, chunk_decay = decay_consts[hd]
            state = state_ref[hd]
            for c in range(n_chunks):
                b["p"][hd, c] = (b["scores"][hd, c] * dmat).astype(BF16)
                b["state_in"][hd, c] = state.astype(BF16)
                state = state * chunk_decay + b["kv"][hd, c]
            state_ref[hd] = state

    def retention_outputs(i):
        b = st[i]
        r0 = i * ROW_BLOCK
        for hd, lanes in enumerate(head_lanes):
            xi = decay_consts[hd][2]
            gn = gng_ref[:, lanes]
            for c in range(n_chunks):
                rows = chunk_rows(c)
                intra = _dot(b["p"][hd, c], b["v"][rows, lanes])
                cross = _dot(b["qc"][hd, c], b["state_in"][hd, c]) * xi
                ret = intra + cross
                ms = jnp.mean(ret * ret, axis=-1, keepdims=True)
                yn = ret * lax.rsqrt(ms + EPS) * gn
                gc = b["g"][rows, lanes]
                ybuf_ref[r0 + c * CHUNK:r0 + (c + 1) * CHUNK, lanes] = (
                    gc * jax.nn.sigmoid(gc) * yn).astype(BF16)

    def pooling(i):
        r0 = i * ROW_BLOCK
        t_next = (s * T + r0 + 1 + lax.broadcasted_iota(jnp.int32, (ROW_BLOCK, POOL_GROUP), 0)
                  ).astype(F32)
        for gi, wlen in enumerate(POOL_WINDOWS):
            lanes = slice(gi * POOL_GROUP, (gi + 1) * POOL_GROUP)
            ext = ubuf_ref[r0:r0 + ROW_BLOCK + POOL_TAIL, lanes]
            wsum = _trailing_window_sum(ext, wlen)[POOL_TAIL:]
            inv_count = 1.0 / jnp.minimum(t_next, float(wlen))
            pooled = (wsum * inv_count - ext[POOL_TAIL:]).astype(BF16)
            og = _dot(pooled, pw_ref[gi]) * ps_ref[:, lanes]
            ybuf_ref[rows_of(i), R + gi * POOL_GROUP:R + (gi + 1) * POOL_GROUP] = og.astype(BF16)

    def output_pool_half(i):
        o_ref[0, rows_of(i)] = x_ref[0, rows_of(i)] + _dot(ybuf_ref[rows_of(i), R:], wout_ref[R:, :])

    def output_retention_half(i):
        o_ref[0, rows_of(i)] += _dot(ybuf_ref[rows_of(i), :R], wout_ref[:R, :])

    project_qk(0)
    project_rest(0)
    retention_scores(0)
    for i in range(n_blocks):
        has_next = i + 1 < n_blocks
        if has_next:
            project_qk(i + 1)
        pooling(i)
        retention_states(i)
        retention_outputs(i)
        if has_next:
            project_rest(i + 1)
            retention_scores(i + 1)
        if i > 0:
            output_retention_half(i - 1)
        output_pool_half(i)
    output_retention_half(n_blocks - 1)

    ubuf_ref[0:POOL_TAIL, :] = ubuf_ref[T:T + POOL_TAIL, :]


def _mixer(x, cosf, sinf, l, mix_norm_g, w_in_b, ret_gn_g, pool_w_b, pool_scale, w_out_b):
    B, S, D = x.shape
    T = MIX_TILE
    tile = pl.BlockSpec((1, T, D), lambda b, s: (b, s, 0))
    rope = pl.BlockSpec((1, T, RET_HEAD_DIM), lambda b, s: (b, s, 0))
    return pl.pallas_call(
        functools.partial(_mix_kernel, T=T),
        grid=(B, S // T),
        in_specs=[
            tile, rope, rope,
            _row_spec(D, l),
            _const_spec((None, D, w_in_b.shape[-1]), (l, 0, 0)),
            _row_spec(RET_WIDTH, l),
            _const_spec((None, len(POOL_WINDOWS), POOL_GROUP, POOL_GROUP), (l, 0, 0, 0)),
            _row_spec(POOL_WIDTH, l),
            _const_spec((None, D, D), (l, 0, 0)),
        ],
        out_specs=tile,
        out_shape=jax.ShapeDtypeStruct(x.shape, F32),
        scratch_shapes=[
            pltpu.VMEM((RET_HEADS, RET_HEAD_DIM, RET_HEAD_DIM), F32),
            pltpu.VMEM((T + POOL_TAIL, POOL_WIDTH), F32),
            pltpu.VMEM((T, D), BF16),
        ],
        compiler_params=pltpu.CompilerParams(
            dimension_semantics=("parallel", "arbitrary"),
            vmem_limit_bytes=VMEM_LIMIT_BYTES),
        name=f"mixer_l{l}",
    )(x, cosf, sinf, mix_norm_g, w_in_b, ret_gn_g, pool_w_b, pool_scale, w_out_b)


def _xa_kernel(x_ref, ng_ref, wq_ref, kt_ref, v_ref, wo_ref, o_ref, obuf_ref, *, T):
    n_blocks = T // ROW_BLOCK
    heads = [slice(hd * XA_HEAD_DIM, (hd + 1) * XA_HEAD_DIM) for hd in range(XA_HEADS)]
    scores = [None] * n_blocks

    def rows_of(i):
        return slice(i * ROW_BLOCK, (i + 1) * ROW_BLOCK)

    def score(i):
        h = _rmsnorm(x_ref[0, rows_of(i)], ng_ref[...]).astype(BF16)
        q = _dot(h, wq_ref[...])
        scores[i] = [_dot(q[:, dims].astype(BF16), kt_ref[0, dims, :]) * (XA_HEAD_DIM ** -0.5)
                     for dims in heads]

    def attend(i):
        for dims, sc in zip(heads, scores[i]):
            p = jnp.exp(sc - jnp.max(sc, axis=-1, keepdims=True))
            p = p * (1.0 / jnp.sum(p, axis=-1, keepdims=True))
            obuf_ref[rows_of(i), dims] = _dot(p.astype(BF16), v_ref[0, :, dims]).astype(BF16)

    def project_out(i):
        o_ref[0, rows_of(i)] = x_ref[0, rows_of(i)] + _dot(obuf_ref[rows_of(i), :], wo_ref[...])

    score(0)
    for i in range(n_blocks):
        if i + 1 < n_blocks:
            score(i + 1)
        attend(i)
        project_out(i)


def _cross_attention(x, l, xa_norm_g, wq_b, kt, v, wo_b):
    B, S, D = x.shape
    T = XA_TILE
    tile = pl.BlockSpec((1, T, D), lambda b, s: (b, s, 0))
    return pl.pallas_call(
        functools.partial(_xa_kernel, T=T),
        grid=(B, S // T),
        in_specs=[
            tile,
            _row_spec(D, l),
            _const_spec((None, D, D), (l, 0, 0)),
            pl.BlockSpec((None, 1, D, N_MEM), lambda b, s: (l, b, 0, 0)),
            pl.BlockSpec((None, 1, N_MEM, D), lambda b, s: (l, b, 0, 0)),
            _const_spec((None, D, D), (l, 0, 0)),
        ],
        out_specs=tile,
        out_shape=jax.ShapeDtypeStruct(x.shape, F32),
        scratch_shapes=[pltpu.VMEM((T, D), BF16)],
        compiler_params=pltpu.CompilerParams(
            dimension_semantics=("parallel", "arbitrary"),
            vmem_limit_bytes=VMEM_LIMIT_BYTES),
        name=f"cross_attn_l{l}",
    )(x, xa_norm_g, wq_b, kt, v, wo_b)


def _shift_rows(a, prev, k):
    rolled = pltpu.roll(a, k, 0)
    row = lax.broadcasted_iota(jnp.int32, prev.shape, 0)
    head = jnp.where(row < k, pltpu.roll(prev, k, 0), rolled[0:CONV_TAIL])
    return jnp.concatenate([head, rolled[CONV_TAIL:]], axis=0)


def _causal_conv(a, tail_ref, cw):
    prev = tail_ref[...]
    tail_ref[...] = a[a.shape[0] - CONV_TAIL:]
    return (cw[0:1] * _shift_rows(a, prev, 2) + cw[1:2] * _shift_rows(a, prev, 1)
            + cw[2:3] * a + cw[3:4])


def _ffn_kernel(x_ref, ng_ref, wup_ref, cw_ref, wd_ref, fg_ref, o_ref,
                hbuf_ref, act_ref, tail_ref, *, final):
    @pl.when(pl.program_id(1) == 0)
    def _():
        tail_ref[...] = jnp.zeros_like(tail_ref)

    hbuf_ref[...] = _rmsnorm(x_ref[0], ng_ref[...]).astype(BF16)

    for j in range(N_FF_BLOCKS):
        h = hbuf_ref[...]
        gate = slice(j * FF_BLOCK, (j + 1) * FF_BLOCK)
        val = slice(D_FF + j * FF_BLOCK, D_FF + (j + 1) * FF_BLOCK)
        cg = _causal_conv(_dot(h, wup_ref[:, gate]), tail_ref.at[j], cw_ref[:, gate])
        cv = _causal_conv(_dot(h, wup_ref[:, val]), tail_ref.at[N_FF_BLOCKS + j], cw_ref[:, val])
        act_ref[:, gate] = (cg * jax.nn.sigmoid(cg) * cv).astype(BF16)

    o_ref[0] = x_ref[0] + _dot(act_ref[...], wd_ref[...])
    if final:
        o_ref[0] = _rmsnorm(o_ref[0], fg_ref[...])


def _ffn(x, l, ffn_norm_g, wup_b, conv_taps, wd_b, final_g, final):
    B, S, D = x.shape
    T = FFN_TILE
    NB = N_FF_BLOCKS
    tile = pl.BlockSpec((1, T, D), lambda b, s: (b, s, 0))
    return pl.pallas_call(
        functools.partial(_ffn_kernel, final=final),
        grid=(B, S // T),
        in_specs=[
            tile,
            _row_spec(D, l),
            _const_spec((None, D, 2 * D_FF), (l, 0, 0)),
            _const_spec((None, CONV_TAIL, 2 * D_FF), (l, 0, 0)),
            _const_spec((None, D_FF, D), (l, 0, 0)),
            _const_spec((1, D), (0, 0)),
        ],
        out_specs=tile,
        out_shape=jax.ShapeDtypeStruct(x.shape, F32),
        scratch_shapes=[
            pltpu.VMEM((T, D), BF16),
            pltpu.VMEM((T, D_FF), BF16),
            pltpu.VMEM((2 * NB, CONV_TAIL, FF_BLOCK), F32),
        ],
        compiler_params=pltpu.CompilerParams(
            dimension_semantics=("parallel", "arbitrary"),
            vmem_limit_bytes=VMEM_LIMIT_BYTES),
        name=f"ffn_l{l}",
    )(x, ffn_norm_g, wup_b, conv_taps, wd_b, final_g)


def _conv_taps(ffn_conv_w, ffn_conv_b):
    taps = jnp.concatenate([ffn_conv_w, ffn_conv_b[:, None, :]], axis=1)
    return jnp.pad(taps, ((0, 0), (0, CONV_TAIL - taps.shape[1]), (0, 0)))


def kernel(x, mem, positions, mix_norm_g, w_in, ret_gn_g, pool_w, pool_scale, w_out, xa_norm_g,
           mem_norm_g, xa_wq, xa_wkv, xa_wo, ffn_norm_g, ffn_w_up, ffn_conv_w, ffn_conv_b,
           ffn_w_down, final_norm_g):
    L = w_in.shape[0]
    S = x.shape[1]
    assert S % MIX_TILE == 0 and S % XA_TILE == 0 and S % FFN_TILE == 0
    assert MIX_TILE % ROW_BLOCK == 0 and XA_TILE % ROW_BLOCK == 0 and ROW_BLOCK % CHUNK == 0
    w_in_b, w_out_b, pool_w_b = w_in.astype(BF16), w_out.astype(BF16), pool_w.astype(BF16)
    wq_b, wkv_b, wo_b = xa_wq.astype(BF16), xa_wkv.astype(BF16), xa_wo.astype(BF16)
    wup_b, wd_b = ffn_w_up.astype(BF16), ffn_w_down.astype(BF16)
    conv_taps = _conv_taps(ffn_conv_w, ffn_conv_b)
    final_g = final_norm_g.reshape(1, D_MODEL)
    mix_norm_g, ret_gn_g, pool_scale = _rows(mix_norm_g), _rows(ret_gn_g), _rows(pool_scale)
    xa_norm_g, ffn_norm_g = _rows(xa_norm_g), _rows(ffn_norm_g)

    cosf, sinf = _rope_tables(positions)
    kt, v = _memory_kv(mem, mem_norm_g, wkv_b)
    for l in range(L):
        x = _mixer(x, cosf, sinf, l, mix_norm_g, w_in_b, ret_gn_g, pool_w_b, pool_scale, w_out_b)
        x = _cross_attention(x, l, xa_norm_g, wq_b, kt, v, wo_b)
        x = _ffn(x, l, ffn_norm_g, wup_b, conv_taps, wd_b, final_g, final=(l == L - 1))
    return x
```
